```python
import jax, jax.numpy as jnp
from jax import lax
import numpy as np

D_MODEL = 4096
BATCH = 4
SEQ = 2048
DEPTH = 4
DEC_BATCH = 8
DEC_SEQ = 1
PAST_LEN = 8192
PAGE_SIZE = 128

HEAD_DIM = 128
ATTN_SCALE = HEAD_DIM ** -0.5
N_MIXERS = 2
N_A_LAYERS = (DEPTH + N_MIXERS - 1) // N_MIXERS
N_B_LAYERS = DEPTH // N_MIXERS
A_WINDOWS = (128, 512, 2048)
A_DILATIONS = (1, 4, 16)
A_N_GROUPS = 3
A_STRIDED_KEYS = 128
A_GROUP_HEADS = D_MODEL // HEAD_DIM // 2
A_IN_COLS = A_N_GROUPS * 3 * A_GROUP_HEADS * HEAD_DIM
A_OUT_ROWS = A_GROUP_HEADS * HEAD_DIM
BLOCK = 128
B_HEADS = D_MODEL // HEAD_DIM
B_IN_COLS = 3 * B_HEADS * HEAD_DIM
B_OUT_ROWS = B_HEADS * HEAD_DIM
B_LOGIT_BIAS_INIT = -9.0
MEM_LEN = 256
MEM_HEADS = 4
MEM_WIDTH = MEM_HEADS * HEAD_DIM
N_EXPERTS = 32
TOP_K = 4
EXPERT_FF = D_MODEL // 4
SWIGLU_LIMIT = 7.0
SWIGLU_ALPHA = 1.702
ROPE_THETA = 10000.0
LN_EPS = 1e-5
DEEPNORM_ALPHA = (2 * DEPTH) ** 0.25
DEEPNORM_BETA = (8 * DEPTH) ** -0.25

kernel_name = 'hybrid_dilated_stickbreaking_moe_decode_step'


def layer_norm(x, g, b):
    xf = x.astype(jnp.float32)
    mu = jnp.mean(xf, axis=-1, keepdims=True)
    var = jnp.mean(jnp.square(xf - mu), axis=-1, keepdims=True)
    y = (xf - mu) * lax.rsqrt(var + LN_EPS) * g.astype(jnp.float32) + b.astype(jnp.float32)
    return y.astype(x.dtype)


def deepnorm(x, fx, g, b):
    return layer_norm(DEEPNORM_ALPHA * x + fx, g, b)


def rope(x, pos):
    half = HEAD_DIM // 2
    inv_freq = ROPE_THETA ** (-jnp.arange(half, dtype=jnp.float32) / half)
    ang = pos.astype(jnp.float32)[:, None] * inv_freq[None, :]
    cos = jnp.cos(ang)[None, :, None, :]
    sin = jnp.sin(ang)[None, :, None, :]
    xf = x.astype(jnp.float32)
    x1, x2 = xf[..., :half], xf[..., half:]
    return jnp.concatenate([x1 * cos - x2 * sin, x2 * cos + x1 * sin], axis=-1).astype(x.dtype)


def a_project(x, w_in, pos):
    b, t, _ = x.shape
    qkv = (x @ w_in).reshape(b, t, A_N_GROUPS, 3, A_GROUP_HEADS, HEAD_DIM)
    return [(rope(qkv[:, :, g, 0], pos), rope(qkv[:, :, g, 1], pos), qkv[:, :, g, 2]) for g in range(A_N_GROUPS)]


def dilated_group_prompt(q, k, v, dil):
    bsz, s, h, dh = q.shape
    n_sub = s // dil
    nb = -(-n_sub // BLOCK)
    pad = nb * BLOCK - n_sub

    def strided(x):
        x = x.reshape(bsz, n_sub, dil, h, dh)
        return jnp.pad(x, ((0, 0), (0, pad), (0, 0), (0, 0), (0, 0)))

    def banded(x):
        x = jnp.pad(strided(x), ((0, 0), (BLOCK, 0), (0, 0), (0, 0), (0, 0)))
        x = x.reshape(bsz, nb + 1, BLOCK, dil, h, dh)
        return jnp.concatenate([x[:, :-1], x[:, 1:]], axis=2)

    qb = strided(q).reshape(bsz, nb, BLOCK, dil, h, dh)
    kb, vb = banded(k), banded(v)
    sc = jnp.einsum('bnqrhd,bnkrhd->bnqrhk', qb, kb, preferred_element_type=jnp.float32) * ATTN_SCALE
    qi = jnp.arange(BLOCK)[:, None]
    kc = jnp.arange(2 * BLOCK)[None, :]
    dist = BLOCK + qi - kc
    blk = jnp.arange(nb)[:, None, None]
    valid = (dist >= 0) & (dist <= A_STRIDED_KEYS) & ((blk > 0) | (kc >= BLOCK))
    sc = jnp.where(valid[None, :, :, None, None, :], sc, -jnp.inf)
    m = jnp.max(sc, axis=-1)
    p = jnp.exp(sc - m[..., None])
    l = jnp.sum(p, axis=-1)
    n = jnp.einsum('bnqrhk,bnkrhd->bnqrhd', p, vb.astype(jnp.float32))

    def unstride(x):
        x = x.reshape(bsz, nb * BLOCK, dil, *x.shape[4:])[:, :n_sub]
        return x.reshape(bsz, s, *x.shape[3:])

    return unstride(m), unstride(l), unstride(n)


def dilated_group_step(q, k_new, v_new, k_buf, v_buf, dil):
    t = q.shape[1]
    wb = k_buf.shape[1]
    k_all = jnp.concatenate([k_buf, k_new], axis=1)
    v_all = jnp.concatenate([v_buf, v_new], axis=1)
    idx = wb + jnp.arange(t)[:, None] - dil * jnp.arange(A_STRIDED_KEYS + 1)[None, :]
    valid = idx >= 0
    idx = jnp.maximum(idx, 0)
    kg = k_all[:, idx]
    vg = v_all[:, idx]
    sc = jnp.einsum('bthd,btkhd->bthk', q, kg, preferred_element_type=jnp.float32) * ATTN_SCALE
    sc = jnp.where(valid[None, :, None, :], sc, -jnp.inf)
    m = jnp.max(sc, axis=-1)
    p = jnp.exp(sc - m[..., None])
    l = jnp.sum(p, axis=-1)
    n = jnp.einsum('bthk,btkhd->bthd', p, vg.astype(jnp.float32))
    return m, l, n


def combine_groups(stats):
    ms = jnp.stack([st[0] for st in stats])
    ls = jnp.stack([st[1] for st in stats])
    ns = jnp.stack([st[2] for st in stats])
    w = jnp.exp(ms - jnp.max(ms, axis=0, keepdims=True))
    num = jnp.sum(w[..., None] * ns, axis=0)
    den = jnp.sum(w * ls, axis=0)
    return num / den[..., None]


def mixer_a_prompt(x, w_in, w_out):
    b, s, _ = x.shape
    groups = a_project(x, w_in, jnp.arange(s))
    stats = [dilated_group_prompt(q, k, v, dil) for (q, k, v), dil in zip(groups, A_DILATIONS)]
    o = combine_groups(stats).reshape(b, s, A_OUT_ROWS).astype(x.dtype)
    rows_k = [grp[1][:, s - min(w, s):] for grp, w in zip(groups, A_WINDOWS)]
    rows_v = [grp[2][:, s - min(w, s):] for grp, w in zip(groups, A_WINDOWS)]
    return o @ w_out, rows_k, rows_v


def mixer_a_step(x, w_in, w_out, bufs_k, bufs_v):
    b, t, _ = x.shape
    groups = a_project(x, w_in, PAST_LEN + jnp.arange(t))
    stats = [dilated_group_step(q, k, v, kb, vb, dil)
             for (q, k, v), kb, vb, dil in zip(groups, bufs_k, bufs_v, A_DILATIONS)]
    o = combine_groups(stats).reshape(b, t, A_OUT_ROWS).astype(x.dtype)
    return o @ w_out, [grp[1] for grp in groups], [grp[2] for grp in groups]


def stick_breaking(q, k, v, bias, q_pos, k_pos):
    z = jnp.einsum('bqhd,bkhd->bhqk', q, k, preferred_element_type=jnp.float32) * ATTN_SCALE
    z = z + bias.astype(jnp.float32)[None, :, None, None]
    causal = (k_pos[None, :] < q_pos[:, None])[None, None]
    log_keep = jnp.where(causal, jax.nn.log_sigmoid(-z), 0.0)
    after = lax.cumsum(log_keep, axis=3, reverse=True)
    between = jnp.concatenate([after[..., 1:], jnp.zeros_like(after[..., :1])], axis=-1)
    a = jnp.where(causal, jnp.exp(jax.nn.log_sigmoid(z) + between), 0.0)
    return jnp.einsum('bhqk,bkhd->bqhd', a, v.astype(jnp.float32))


def mixer_b_prompt(x, w_in, w_out, bias):
    b, s, _ = x.shape
    qkv = (x @ w_in).reshape(b, s, 3, B_HEADS, HEAD_DIM)
    q, k, v = qkv[:, :, 0], qkv[:, :, 1], qkv[:, :, 2]
    nb = s // BLOCK
    pos = jnp.arange(s)
    q_blocks = q.reshape(b, nb, BLOCK, B_HEADS, HEAD_DIM).transpose(1, 0, 2, 3, 4)
    pos_blocks = pos.reshape(nb, BLOCK)
    o = lax.map(lambda qp: stick_breaking(qp[0], k, v, bias, qp[1], pos), (q_blocks, pos_blocks))
    o = o.transpose(1, 0, 2, 3, 4).reshape(b, s, B_OUT_ROWS).astype(x.dtype)
    return o @ w_out, k, v


def mixer_b_step(x, w_in, w_out, bias, k_pool, v_pool, page_table):
    b, t, _ = x.shape
    qkv = (x @ w_in).reshape(b, t, 3, B_HEADS, HEAD_DIM)
    q, k, v = qkv[:, :, 0], qkv[:, :, 1], qkv[:, :, 2]
    past = page_table.shape[1] * PAGE_SIZE
    k_past = k_pool[page_table].reshape(b, past, B_HEADS, HEAD_DIM)
    v_past = v_pool[page_table].reshape(b, past, B_HEADS, HEAD_DIM)
    k_all = jnp.concatenate([k_past, k], axis=1)
    v_all = jnp.concatenate([v_past, v], axis=1)
    o = stick_breaking(q, k_all, v_all, bias, past + jnp.arange(t), jnp.arange(past + t))
    o = o.reshape(b, t, B_OUT_ROWS).astype(x.dtype)
    return o @ w_out, k, v


def memory_kv(mem, w_kv):
    b, m, _ = mem.shape
    kv = (mem @ w_kv).reshape(b, m, 2, MEM_HEADS, HEAD_DIM)
    return kv[:, :, 0], kv[:, :, 1]


def memory_attend(x, k, v, w_q, w_out):
    b, t, _ = x.shape
    q = (x @ w_q).reshape(b, t, MEM_HEADS, HEAD_DIM)
    sc = jnp.einsum('bthd,bmhd->bhtm', q, k, preferred_element_type=jnp.float32) * ATTN_SCALE
    p = jax.nn.softmax(sc, axis=-1)
    o = jnp.einsum('bhtm,bmhd->bthd', p, v.astype(jnp.float32)).reshape(b, t, MEM_WIDTH).astype(x.dtype)
    return o @ w_out


def moe(xt, w_r, b_r, w_gu, b_gu, w_d, b_d):
    logits = jnp.matmul(xt, w_r, preferred_element_type=jnp.float32) + b_r.astype(jnp.float32)
    top_val, top_idx = lax.top_k(logits, TOP_K)
    top_w = jax.nn.softmax(top_val, axis=-1)
    gates = jnp.einsum('nk,nke->ne', top_w, jax.nn.one_hot(top_idx, N_EXPERTS, dtype=jnp.float32)).astype(xt.dtype)
    y = jnp.zeros_like(xt)
    for e in range(N_EXPERTS):
        gu = xt @ w_gu[e] + b_gu[e]
        glu = jnp.minimum(gu[:, :EXPERT_FF], SWIGLU_LIMIT)
        lin = jnp.clip(gu[:, EXPERT_FF:], -SWIGLU_LIMIT, SWIGLU_LIMIT)
        h = glu * jax.nn.sigmoid(SWIGLU_ALPHA * glu) * (lin + 1.0)
        y = y + gates[:, e:e + 1] * (h @ w_d[e] + b_d[e])
    return y


def setup_inputs(seed: int = 0) -> dict:
    key = jax.random.key(seed)
    ks = jax.random.split(key, 40)
    f32 = jnp.float32

    def nrm(k, shape, scale=1.0):
        return jax.random.normal(k, shape, f32) * scale

    n_pages = PAST_LEN // PAGE_SIZE
    n_used = DEC_BATCH * n_pages
    n_pool = n_used + max(1, n_used // 4)
    page_table = jax.random.permutation(ks[0], n_pool)[:n_used].reshape(DEC_BATCH, n_pages).astype(jnp.int32)
    wb = [min(w, PAST_LEN) for w in A_WINDOWS]
    return {
        'x_prompt': nrm(ks[1], (BATCH, SEQ, D_MODEL)),
        'x_sample': nrm(ks[2], (DEC_BATCH, DEC_SEQ, D_MODEL)),
        'cache_a1_k': nrm(ks[3], (N_A_LAYERS, DEC_BATCH, wb[0], A_GROUP_HEADS, HEAD_DIM)),
        'cache_a1_v': nrm(ks[4], (N_A_LAYERS, DEC_BATCH, wb[0], A_GROUP_HEADS, HEAD_DIM)),
        'cache_a2_k': nrm(ks[5], (N_A_LAYERS, DEC_BATCH, wb[1], A_GROUP_HEADS, HEAD_DIM)),
        'cache_a2_v': nrm(ks[6], (N_A_LAYERS, DEC_BATCH, wb[1], A_GROUP_HEADS, HEAD_DIM)),
        'cache_a3_k': nrm(ks[7], (N_A_LAYERS, DEC_BATCH, wb[2], A_GROUP_HEADS, HEAD_DIM)),
        'cache_a3_v': nrm(ks[8], (N_A_LAYERS, DEC_BATCH, wb[2], A_GROUP_HEADS, HEAD_DIM)),
        'cache_b_k': nrm(ks[9], (N_B_LAYERS, n_pool, PAGE_SIZE, B_HEADS, HEAD_DIM)),
        'cache_b_v': nrm(ks[10], (N_B_LAYERS, n_pool, PAGE_SIZE, B_HEADS, HEAD_DIM)),
        'cache_mem_k': nrm(ks[11], (DEPTH, DEC_BATCH, MEM_LEN, MEM_HEADS, HEAD_DIM)),
        'cache_mem_v': nrm(ks[12], (DEPTH, DEC_BATCH, MEM_LEN, MEM_HEADS, HEAD_DIM)),
        'page_table': page_table,
        'mem_prompt': nrm(ks[13], (BATCH, MEM_LEN, D_MODEL)),
        'a_w_in': nrm(ks[14], (N_A_LAYERS, D_MODEL, A_IN_COLS), D_MODEL ** -0.5),
        'a_w_out': nrm(ks[15], (N_A_LAYERS, A_OUT_ROWS, D_MODEL), DEEPNORM_BETA * A_OUT_ROWS ** -0.5),
        'b_w_in': nrm(ks[16], (N_B_LAYERS, D_MODEL, B_IN_COLS), D_MODEL ** -0.5),
        'b_w_out': nrm(ks[17], (N_B_LAYERS, B_OUT_ROWS, D_MODEL), DEEPNORM_BETA * B_OUT_ROWS ** -0.5),
        'b_logit_bias': B_LOGIT_BIAS_INIT + nrm(ks[29], (N_B_LAYERS, B_HEADS), 0.1),
        'mem_w_q': nrm(ks[18], (DEPTH, D_MODEL, MEM_WIDTH), D_MODEL ** -0.5),
        'mem_w_kv': nrm(ks[19], (DEPTH, D_MODEL, 2 * MEM_WIDTH), D_MODEL ** -0.5),
        'mem_w_out': nrm(ks[20], (DEPTH, MEM_WIDTH, D_MODEL), DEEPNORM_BETA * MEM_WIDTH ** -0.5),
        'ln_g': 1.0 + nrm(ks[21], (DEPTH, 3, D_MODEL), 0.02),
        'ln_b': nrm(ks[22], (DEPTH, 3, D_MODEL), 0.02),
        'router_w': nrm(ks[23], (DEPTH, D_MODEL, N_EXPERTS), D_MODEL ** -0.5),
        'router_b': nrm(ks[24], (DEPTH, N_EXPERTS), 0.01),
        'moe_w_gu': nrm(ks[25], (DEPTH, N_EXPERTS, D_MODEL, 2 * EXPERT_FF), D_MODEL ** -0.5),
        'moe_b_gu': nrm(ks[26], (DEPTH, N_EXPERTS, 2 * EXPERT_FF), 0.02),
        'moe_w_down': nrm(ks[27], (DEPTH, N_EXPERTS, EXPERT_FF, D_MODEL), DEEPNORM_BETA * EXPERT_FF ** -0.5),
        'moe_b_down': nrm(ks[28], (DEPTH, N_EXPERTS, D_MODEL), 0.02),
    }


def reference(x_prompt, x_sample, cache_a1_k, cache_a1_v, cache_a2_k, cache_a2_v, cache_a3_k, cache_a3_v,
              cache_b_k, cache_b_v, cache_mem_k, cache_mem_v, page_table, mem_prompt,
              a_w_in, a_w_out, b_w_in, b_w_out, b_logit_bias, mem_w_q, mem_w_kv, mem_w_out, ln_g, ln_b,
              router_w, router_b, moe_w_gu, moe_b_gu, moe_w_down, moe_b_down):
    bp, sp, _ = x_prompt.shape
    bs, ts, _ = x_sample.shape
    win_k = (cache_a1_k, cache_a2_k, cache_a3_k)
    win_v = (cache_a1_v, cache_a2_v, cache_a3_v)
    pa_k = [[] for _ in range(A_N_GROUPS)]
    pa_v = [[] for _ in range(A_N_GROUPS)]
    sa_k = [[] for _ in range(A_N_GROUPS)]
    sa_v = [[] for _ in range(A_N_GROUPS)]
    pb_k, pb_v, sb_k, sb_v, pm_k, pm_v = [], [], [], [], [], []
    xp, xs = x_prompt, x_sample
    for i in range(DEPTH):
        j = i // N_MIXERS
        if i % N_MIXERS == 0:
            fp, rk, rv = mixer_a_prompt(xp, a_w_in[j], a_w_out[j])
            fs, nk, nv = mixer_a_step(xs, a_w_in[j], a_w_out[j], [c[j] for c in win_k], [c[j] for c in win_v])
            for g in range(A_N_GROUPS):
                pa_k[g].append(rk[g])
                pa_v[g].append(rv[g])
                sa_k[g].append(nk[g])
                sa_v[g].append(nv[g])
        else:
            fp, kp, vp = mixer_b_prompt(xp, b_w_in[j], b_w_out[j], b_logit_bias[j])
            fs, kn, vn = mixer_b_step(xs, b_w_in[j], b_w_out[j], b_logit_bias[j],
                                      cache_b_k[j], cache_b_v[j], page_table)
            pb_k.append(kp)
            pb_v.append(vp)
            sb_k.append(kn)
            sb_v.append(vn)
        xp = deepnorm(xp, fp, ln_g[i, 0], ln_b[i, 0])
        xs = deepnorm(xs, fs, ln_g[i, 0], ln_b[i, 0])
        mk, mv = memory_kv(mem_prompt, mem_w_kv[i])
        pm_k.append(mk)
        pm_v.append(mv)
        xp = deepnorm(xp, memory_attend(xp, mk, mv, mem_w_q[i], mem_w_out[i]), ln_g[i, 1], ln_b[i, 1])
        xs = deepnorm(xs, memory_attend(xs, cache_mem_k[i], cache_mem_v[i], mem_w_q[i], mem_w_out[i]),
                      ln_g[i, 1], ln_b[i, 1])
        xt = jnp.concatenate([xp.reshape(bp * sp, D_MODEL), xs.reshape(bs * ts, D_MODEL)], axis=0)
        ft = moe(xt, router_w[i], router_b[i], moe_w_gu[i], moe_b_gu[i], moe_w_down[i], moe_b_down[i])
        xt = deepnorm(xt, ft, ln_g[i, 2], ln_b[i, 2])
        xp = xt[:bp * sp].reshape(bp, sp, D_MODEL)
        xs = xt[bp * sp:].reshape(bs, ts, D_MODEL)
    p_a1_k, p_a1_v = jnp.stack(pa_k[0]), jnp.stack(pa_v[0])
    p_a2_k, p_a2_v = jnp.stack(pa_k[1]), jnp.stack(pa_v[1])
    p_a3_k, p_a3_v = jnp.stack(pa_k[2]), jnp.stack(pa_v[2])
    p_b_k, p_b_v = jnp.stack(pb_k), jnp.stack(pb_v)
    p_mem_k, p_mem_v = jnp.stack(pm_k), jnp.stack(pm_v)
    s_a1_k, s_a1_v = jnp.stack(sa_k[0]), jnp.stack(sa_v[0])
    s_a2_k, s_a2_v = jnp.stack(sa_k[1]), jnp.stack(sa_v[1])
    s_a3_k, s_a3_v = jnp.stack(sa_k[2]), jnp.stack(sa_v[2])
    s_b_k, s_b_v = jnp.stack(sb_k), jnp.stack(sb_v)
    return (xp, xs, p_a1_k, p_a1_v, p_a2_k, p_a2_v, p_a3_k, p_a3_v, p_b_k, p_b_v, p_mem_k, p_mem_v,
            s_a1_k, s_a1_v, s_a2_k, s_a2_v, s_a3_k, s_a3_v, s_b_k, s_b_v)
```

```python
import functools

import jax
import jax.numpy as jnp
from jax import lax
from jax.experimental import pallas as pl
from jax.experimental.pallas import tpu as pltpu

F32 = jnp.float32
BF16 = jnp.bfloat16
I32 = jnp.int32

HEAD_DIM = 128
ATTN_SCALE = HEAD_DIM ** -0.5
TOP_K = 4
SWIGLU_LIMIT = 7.0
SWIGLU_ALPHA = 1.702
ROPE_THETA = 10000.0
LN_EPS = 1e-5
A_WINDOWS = (128, 512, 2048)
A_DILATIONS = (1, 4, 16)
A_KEYS = 128
BLOCK = 128
PAGE_SIZE = 128

V7X_VMEM_BYTES = 64 * 1024 * 1024
VMEM_LIMIT = (V7X_VMEM_BYTES * 3) // 4
MOE_TM = 256


def _params(n_axes):
    return pltpu.CompilerParams(dimension_semantics=("arbitrary",) * n_axes,
                                vmem_limit_bytes=VMEM_LIMIT)


def _matmul_kernel(x_ref, w_ref, o_ref, wb_ref):
    @pl.when(pl.program_id(1) == 0)
    def _cast():
        wb_ref[...] = w_ref[...].astype(BF16)

    o_ref[...] = jnp.dot(x_ref[...].astype(BF16), wb_ref[...], preferred_element_type=F32)


def _matmul(x, w, layer, *, tm, tn, n_cols=None, col0=0):
    m, k = x.shape
    n = w.shape[2] if n_cols is None else n_cols
    assert m % tm == 0 and n % tn == 0 and col0 % tn == 0
    cb = col0 // tn
    return pl.pallas_call(
        _matmul_kernel,
        grid=(n // tn, m // tm),
        in_specs=[pl.BlockSpec((tm, k), lambda j, i: (i, 0)),
                  pl.BlockSpec((None, k, tn), lambda j, i: (layer, 0, cb + j))],
        out_specs=pl.BlockSpec((tm, tn), lambda j, i: (i, j)),
        out_shape=jax.ShapeDtypeStruct((m, n), F32),
        scratch_shapes=[pltpu.VMEM((k, tn), BF16)],
        compiler_params=_params(2),
        name="matmul",
    )(x, w)


def _layer_norm_rows(y, g, b):
    mu = jnp.mean(y, axis=-1, keepdims=True)
    d = y - mu
    var = jnp.mean(d * d, axis=-1, keepdims=True)
    return d * lax.rsqrt(var + LN_EPS) * g + b


BF16_ROWS = 16


def _operand_dtype(rows):
    return BF16 if rows % BF16_ROWS == 0 else F32


def _deepnorm_kernel(alpha, x_ref, f_ref, g_ref, b_ref, o_ref, ob_ref):
    y = _layer_norm_rows(alpha * x_ref[...] + f_ref[...], g_ref[...], b_ref[...])
    o_ref[...] = y
    ob_ref[...] = y.astype(ob_ref.dtype)


def _deepnorm(x, f, g, b, alpha, *, tm):
    m, d = x.shape
    assert m % tm == 0
    row = pl.BlockSpec((tm, d), lambda i: (i, 0))
    vec = pl.BlockSpec((1, d), lambda i: (0, 0))
    return pl.pallas_call(
        functools.partial(_deepnorm_kernel, alpha),
        grid=(m // tm,),
        in_specs=[row, row, vec, vec],
        out_specs=[row, row],
        out_shape=[jax.ShapeDtypeStruct((m, d), F32), jax.ShapeDtypeStruct((m, d), _operand_dtype(m))],
        compiler_params=_params(1),
        name="deepnorm",
    )(x, f, g.reshape(1, d), b.reshape(1, d))


def _router_kernel(n_valid, x_ref, wrt_ref, br_ref, cin_ref,
                   idx_ref, gate_ref, rank_ref, cout_ref, carry_ref):
    i = pl.program_id(0)
    n_exp, tm = wrt_ref.shape[0], x_ref.shape[0]

    @pl.when(i == 0)
    def _init():
        carry_ref[...] = cin_ref[...]

    logits = lax.dot_general(wrt_ref[...], x_ref[...].astype(BF16), (((1,), (1,)), ((), ())),
                             preferred_element_type=F32) + br_ref[...]
    e_iota = lax.broadcasted_iota(I32, (n_exp, tm), 0)
    tok = i * tm + lax.broadcasted_iota(I32, (1, tm), 1)
    live = tok < n_valid
    work = logits
    vals, idxs, hots = [], [], []
    for _ in range(TOP_K):
        m = jnp.max(work, axis=0, keepdims=True)
        sel = jnp.min(jnp.where(work == m, e_iota, n_exp), axis=0, keepdims=True)
        hot = e_iota == sel
        vals.append(m)
        idxs.append(sel)
        hots.append(hot)
        work = jnp.where(hot, -jnp.inf, work)
    exps = [jnp.exp(v - vals[0]) for v in vals]
    denom = exps[0] + exps[1] + exps[2] + exps[3]
    any_hot = jnp.where((hots[0] | hots[1] | hots[2] | hots[3]) & live, 1.0, 0.0)
    earlier = (lax.broadcasted_iota(I32, (tm, tm), 0) < lax.broadcasted_iota(I32, (tm, tm), 1))
    before = jnp.dot(any_hot.astype(BF16), jnp.where(earlier, 1.0, 0.0).astype(BF16),
                     preferred_element_type=F32) + carry_ref[...]
    for k in range(TOP_K):
        idx_ref[k:k + 1, :] = idxs[k]
        gate_ref[k:k + 1, :] = exps[k] / denom
        rank_ref[k:k + 1, :] = jnp.sum(jnp.where(hots[k], before, 0.0), axis=0,
                                       keepdims=True).astype(I32)
    carry_ref[...] = carry_ref[...] + jnp.sum(any_hot, axis=1, keepdims=True)
    cout_ref[...] = carry_ref[...]


def _router(x, wrt, br, counts_in, n_valid, *, tm):
    m, d = x.shape
    n_exp = wrt.shape[0]
    assert m % tm == 0
    full = lambda shape: pl.BlockSpec(shape, lambda i: (0, 0))
    out_row = pl.BlockSpec((TOP_K, tm), lambda i: (0, i))
    return pl.pallas_call(
        functools.partial(_router_kernel, n_valid),
        grid=(m // tm,),
        in_specs=[pl.BlockSpec((tm, d), lambda i: (i, 0)), full((n_exp, d)), full((n_exp, 1)),
                  full((n_exp, 1))],
        out_specs=[out_row, out_row, out_row, full((n_exp, 1))],
        out_shape=[jax.ShapeDtypeStruct((TOP_K, m), I32), jax.ShapeDtypeStruct((TOP_K, m), F32),
                   jax.ShapeDtypeStruct((TOP_K, m), I32), jax.ShapeDtypeStruct((n_exp, 1), F32)],
        scratch_shapes=[pltpu.VMEM((n_exp, 1), F32)],
        compiler_params=_params(1),
        name="router",
    )(x, wrt, br, counts_in)


def _dispatch_kernel(pos_ref, x_ref, xs_in_ref, xs_ref, sem):
    del xs_in_ref
    tm = x_ref.shape[0]

    def row_copy(r, k):
        p = pos_ref[0, 0, k * tm + r]
        return pltpu.make_async_copy(x_ref.at[pl.ds(r, 1)], xs_ref.at[pl.ds(p, 1)], sem)

    def issue(r, carry):
        for k in range(TOP_K):
            row_copy(r, k).start()
        return carry

    lax.fori_loop(0, tm, issue, 0)

    def drain(r, carry):
        for k in range(TOP_K):
            row_copy(r, k).wait()
        return carry

    lax.fori_loop(0, tm, drain, 0)


def _dispatch(x, pos, xs, *, tm):
    m, d = x.shape
    assert m % tm == 0
    nt = m // tm
    pos_t = pos.reshape(TOP_K, nt, tm).transpose(1, 0, 2).reshape(nt, 1, TOP_K * tm)
    return pl.pallas_call(
        _dispatch_kernel,
        grid=(nt,),
        in_specs=[pl.BlockSpec((1, 1, TOP_K * tm), lambda i: (i, 0, 0), memory_space=pltpu.SMEM),
                  pl.BlockSpec((tm, d), lambda i: (i, 0)),
                  pl.BlockSpec(memory_space=pl.ANY)],
        out_specs=pl.BlockSpec(memory_space=pl.ANY),
        out_shape=jax.ShapeDtypeStruct(xs.shape, xs.dtype),
        scratch_shapes=[pltpu.SemaphoreType.DMA(())],
        input_output_aliases={2: 0},
        compiler_params=_params(1),
        name="moe_dispatch",
    )(pos_t, x, xs)


def _tile_is_first(te_ref, i):
    return jnp.logical_or(i == 0, te_ref[i] != te_ref[jnp.maximum(i - 1, 0)])


def _gmm1_kernel(te_ref, tr_ref, x_ref, wg_ref, wl_ref, bg_ref, bl_ref, h_ref, wgb_ref, wlb_ref):
    i = pl.program_id(1)

    @pl.when(_tile_is_first(te_ref, i))
    def _cast():
        wgb_ref[...] = wg_ref[...].astype(BF16)
        wlb_ref[...] = wl_ref[...].astype(BF16)

    rows = tr_ref[i]

    @pl.when(rows > 0)
    def _compute():
        tm = x_ref.shape[0]
        keep = lax.broadcasted_iota(I32, (tm, 1), 0) < rows
        x = jnp.where(keep, x_ref[...], 0.0).astype(BF16)
        gate = jnp.dot(x, wgb_ref[...], preferred_element_type=F32) + bg_ref[...]
        lin = jnp.dot(x, wlb_ref[...], preferred_element_type=F32) + bl_ref[...]
        glu = jnp.minimum(gate, SWIGLU_LIMIT)
        lin = jnp.clip(lin, -SWIGLU_LIMIT, SWIGLU_LIMIT)
        h = glu * jax.nn.sigmoid(SWIGLU_ALPHA * glu) * (lin + 1.0)
        h_ref[...] = h.astype(BF16)

    @pl.when(rows <= 0)
    def _blank():
        h_ref[...] = jnp.zeros(h_ref.shape, BF16)


def _gmm1(xs, w_gu, b_gu, layer, tile_expert, tile_rows, *, tn):
    rows, d = xs.shape
    ff = w_gu.shape[3] // 2
    n_exp = w_gu.shape[1]
    nt = rows // MOE_TM
    assert ff % tn == 0
    nc = ff // tn
    b4 = b_gu.reshape(b_gu.shape[0], n_exp, 1, 2 * ff)
    grid_spec = pltpu.PrefetchScalarGridSpec(
        num_scalar_prefetch=2,
        grid=(nc, nt),
        in_specs=[
            pl.BlockSpec((MOE_TM, d), lambda j, i, te, tr: (i, 0)),
            pl.BlockSpec((None, None, d, tn), lambda j, i, te, tr: (layer, te[i], 0, j)),
            pl.BlockSpec((None, None, d, tn), lambda j, i, te, tr: (layer, te[i], 0, nc + j)),
            pl.BlockSpec((None, None, 1, tn), lambda j, i, te, tr: (layer, te[i], 0, j)),
            pl.BlockSpec((None, None, 1, tn), lambda j, i, te, tr: (layer, te[i], 0, nc + j)),
        ],
        out_specs=pl.BlockSpec((MOE_TM, tn), lambda j, i, te, tr: (i, j)),
        scratch_shapes=[pltpu.VMEM((d, tn), BF16), pltpu.VMEM((d, tn), BF16)],
    )
    return pl.pallas_call(
        _gmm1_kernel,
        grid_spec=grid_spec,
        out_shape=jax.ShapeDtypeStruct((rows, ff), BF16),
        compiler_params=_params(2),
        name="moe_gmm1",
    )(tile_expert, tile_rows, xs, w_gu, w_gu, b4, b4)


def _gmm2_kernel(te_ref, tr_ref, h_ref, w_ref, b_ref, y_ref, wb_ref):
    i = pl.program_id(1)

    @pl.when(_tile_is_first(te_ref, i))
    def _cast():
        wb_ref[...] = w_ref[...].astype(BF16)

    @pl.when(tr_ref[i] > 0)
    def _compute():
        y_ref[...] = jnp.dot(h_ref[...], wb_ref[...], preferred_element_type=F32) + b_ref[...]

    @pl.when(tr_ref[i] <= 0)
    def _blank():
        y_ref[...] = jnp.zeros(y_ref.shape, F32)


def _gmm2(h, w_d, b_d, layer, tile_expert, tile_rows, *, tn):
    rows, ff = h.shape
    n_exp, d = w_d.shape[1], w_d.shape[3]
    nt = rows // MOE_TM
    assert d % tn == 0
    b4 = b_d.reshape(b_d.shape[0], n_exp, 1, d)
    grid_spec = pltpu.PrefetchScalarGridSpec(
        num_scalar_prefetch=2,
        grid=(d // tn, nt),
        in_specs=[
            pl.BlockSpec((MOE_TM, ff), lambda j, i, te, tr: (i, 0)),
            pl.BlockSpec((None, None, ff, tn), lambda j, i, te, tr: (layer, te[i], 0, j)),
            pl.BlockSpec((None, None, 1, tn), lambda j, i, te, tr: (layer, te[i], 0, j)),
        ],
        out_specs=pl.BlockSpec((MOE_TM, tn), lambda j, i, te, tr: (i, j)),
        scratch_shapes=[pltpu.VMEM((ff, tn), BF16)],
    )
    return pl.pallas_call(
        _gmm2_kernel,
        grid_spec=grid_spec,
        out_shape=jax.ShapeDtypeStruct((rows, d), F32),
        compiler_params=_params(2),
        name="moe_gmm2",
    )(tile_expert, tile_rows, h, w_d, b4)


def _combine_kernel(alpha, pos_ref, x_ref, gate_ref, g_ref, b_ref, y_ref, o_ref, ob_ref, buf_ref, sem):
    tm = x_ref.shape[0]

    def row_copy(r, k):
        p = pos_ref[0, 0, k * tm + r]
        return pltpu.make_async_copy(y_ref.at[pl.ds(p, 1)], buf_ref.at[k, pl.ds(r, 1)], sem)

    def issue(r, carry):
        for k in range(TOP_K):
            row_copy(r, k).start()
        return carry

    lax.fori_loop(0, tm, issue, 0)

    def drain(r, carry):
        for k in range(TOP_K):
            row_copy(r, k).wait()
        return carry

    lax.fori_loop(0, tm, drain, 0)

    gates = gate_ref[...]
    f = gates[:, 0:1] * buf_ref[0]
    for k in range(1, TOP_K):
        f = f + gates[:, k:k + 1] * buf_ref[k]
    y = _layer_norm_rows(alpha * x_ref[...] + f, g_ref[...], b_ref[...])
    o_ref[...] = y
    ob_ref[...] = y.astype(ob_ref.dtype)


def _combine(x, gates, pos, ys, g, b, alpha, *, tm):
    m, d = x.shape
    assert m % tm == 0
    nt = m // tm
    pos_t = pos.reshape(TOP_K, nt, tm).transpose(1, 0, 2).reshape(nt, 1, TOP_K * tm)
    row = pl.BlockSpec((tm, d), lambda i: (i, 0))
    vec = pl.BlockSpec((1, d), lambda i: (0, 0))
    return pl.pallas_call(
        functools.partial(_combine_kernel, alpha),
        grid=(nt,),
        in_specs=[pl.BlockSpec((1, 1, TOP_K * tm), lambda i: (i, 0, 0), memory_space=pltpu.SMEM),
                  row, pl.BlockSpec((tm, TOP_K), lambda i: (i, 0)), vec, vec,
                  pl.BlockSpec(memory_space=pl.ANY)],
        out_specs=[row, row],
        out_shape=[jax.ShapeDtypeStruct((m, d), F32), jax.ShapeDtypeStruct((m, d), _operand_dtype(m))],
        scratch_shapes=[pltpu.VMEM((TOP_K, tm, d), F32), pltpu.SemaphoreType.DMA(())],
        compiler_params=_params(1),
        name="moe_combine",
    )(pos_t, x, gates.T, g.reshape(1, d), b.reshape(1, d), ys)


def _moe_layer(xp, xpb, xs, layer, router_w, router_b, w_gu, b_gu, w_d, b_d, g, b, alpha):
    mp, d = xp.shape
    ms = xs.shape[0]
    n_exp = router_w.shape[2]
    wrt = router_w[layer].T.astype(BF16)
    br = router_b[layer].reshape(n_exp, 1)
    pad = 128
    xs_pad = jnp.zeros((pad, d), F32).at[:ms].set(xs)
    idx_p, gate_p, rank_p, cnt_p = _router(xpb, wrt, br, jnp.zeros((n_exp, 1), F32), mp, tm=512)
    idx_s, gate_s, rank_s, cnt = _router(xs_pad, wrt, br, cnt_p, ms, tm=pad)
    idx = jnp.concatenate([idx_p, idx_s[:, :ms]], axis=1)
    rank = jnp.concatenate([rank_p, rank_s[:, :ms]], axis=1)

    counts = cnt[:, 0].astype(I32)
    tiles = (counts + MOE_TM - 1) // MOE_TM
    tile_end = jnp.cumsum(tiles)
    tile_start = tile_end - tiles
    pos = tile_start[idx] * MOE_TM + rank
    nt = ((mp + ms) * TOP_K) // MOE_TM + n_exp
    t_iota = jnp.arange(nt, dtype=I32)
    n_used = tile_end[-1]
    last_used = jnp.maximum(n_used - 1, 0)
    t_eff = jnp.minimum(t_iota, last_used)
    tile_expert = jnp.minimum(jnp.searchsorted(tile_end, t_eff, side="right"), n_exp - 1).astype(I32)
    tile_rows = jnp.clip(counts[tile_expert] - (t_iota - tile_start[tile_expert]) * MOE_TM, 0, MOE_TM)
    tile_rows = jnp.where(t_iota < n_used, tile_rows, 0).astype(I32)

    sorted_rows = jnp.zeros((nt * MOE_TM, d), F32)
    sorted_rows = _dispatch(xp, pos[:, :mp], sorted_rows, tm=256)
    sorted_rows = _dispatch(xs, pos[:, mp:], sorted_rows, tm=ms)
    h = _gmm1(sorted_rows, w_gu, b_gu, layer, tile_expert, tile_rows, tn=min(256, w_d.shape[2]))
    ys = _gmm2(h, w_d, b_d, layer, tile_expert, tile_rows, tn=min(1024, d))
    yp, ypb = _combine(xp, gate_p, pos[:, :mp], ys, g, b, alpha, tm=64)
    ysn, ysb = _combine(xs, gate_s[:, :ms], pos[:, mp:], ys, g, b, alpha, tm=ms)
    return yp, ypb, ysn, ysb


def _rope_tables(pos):
    half = HEAD_DIM // 2
    inv_freq = ROPE_THETA ** (-jnp.arange(half, dtype=F32) / half)
    ang = pos.astype(F32)[..., None] * inv_freq
    cos, sin = jnp.cos(ang), jnp.sin(ang)
    return jnp.concatenate([cos, cos], axis=-1), jnp.concatenate([-sin, sin], axis=-1)


def _rope(x, cos2, sin2):
    width = x.shape[-1]
    half = HEAD_DIM // 2
    axis = x.ndim - 1
    if width == HEAD_DIM:
        return x * cos2 + pltpu.roll(x, half, axis=axis) * sin2
    first = lax.broadcasted_iota(I32, x.shape, axis) % HEAD_DIM < half
    partner = jnp.where(first, pltpu.roll(x, width - half, axis=axis), pltpu.roll(x, half, axis=axis))
    return x * cos2 + partner * sin2


A_HB = 4


def _dilated_kernel(has_prev, *refs):
    if has_prev:
        (q_ref, kc_ref, vc_ref, cc_ref, sc_ref, kp_ref, vp_ref, cp_ref, sp_ref,
         n_ref, st_ref, kr_ref) = refs
    else:
        q_ref, kc_ref, vc_ref, cc_ref, sc_ref, n_ref, st_ref, kr_ref = refs
    ib = pl.program_id(2)
    kw = 2 * BLOCK if has_prev else BLOCK
    qi = lax.broadcasted_iota(I32, (BLOCK, kw), 0)
    kc = lax.broadcasted_iota(I32, (BLOCK, kw), 1)
    dist = (kw - BLOCK) + qi - kc
    valid = (dist >= 0) & (dist <= A_KEYS)
    if has_prev:
        valid = valid & ((ib > 0) | (kc >= BLOCK))
    cos_c, sin_c = cc_ref[...], sc_ref[...]
    lane = lax.broadcasted_iota(I32, (BLOCK, 2 * A_HB), 1)
    stats = jnp.zeros((BLOCK, 2 * A_HB), F32)
    for h in range(A_HB):
        sl = slice(h * HEAD_DIM, (h + 1) * HEAD_DIM)
        q = _rope(q_ref[:, sl], cos_c, sin_c).astype(BF16)
        k_cur = _rope(kc_ref[:, sl], cos_c, sin_c)
        kr_ref[:, sl] = k_cur
        if has_prev:
            k_prev = _rope(kp_ref[:, sl], cp_ref[...], sp_ref[...])
            k_all = jnp.concatenate([k_prev, k_cur], axis=0).astype(BF16)
            v_all = jnp.concatenate([vp_ref[:, sl], vc_ref[:, sl]], axis=0).astype(BF16)
        else:
            k_all = k_cur.astype(BF16)
            v_all = vc_ref[:, sl].astype(BF16)
        s = lax.dot_general(q, k_all, (((1,), (1,)), ((), ())), preferred_element_type=F32) * ATTN_SCALE
        s = jnp.where(valid, s, -jnp.inf)
        m = jnp.max(s, axis=-1, keepdims=True)
        p = jnp.exp(s - m)
        l = jnp.sum(p, axis=-1, keepdims=True)
        n_ref[:, sl] = jnp.dot(p.astype(BF16), v_all, preferred_element_type=F32)
        stats = jnp.where(lane == h, m, stats)
        stats = jnp.where(lane == A_HB + h, l, stats)
    st_ref[...] = stats


def _dilated_prompt(qkv, group, dil, n_heads):
    bsz, s, c = qkv.shape
    n_sub = s // dil
    assert s % dil == 0 and n_sub % BLOCK == 0 and n_heads % A_HB == 0
    nb = n_sub // BLOCK
    has_prev = nb > 1
    hw = A_HB * HEAD_DIM
    gw = n_heads * HEAD_DIM
    nhb = n_heads // A_HB
    cb = c // hw
    q0, k0, v0 = (group * 3 * gw) // hw, ((group * 3 + 1) * gw) // hw, ((group * 3 + 2) * gw) // hw
    view = qkv.reshape(bsz, n_sub, dil * c)
    pos = (jnp.arange(n_sub, dtype=I32)[None, :] * dil + jnp.arange(dil, dtype=I32)[:, None])
    cos2, sin2 = _rope_tables(pos)

    def col(c0):
        return pl.BlockSpec((None, BLOCK, hw), lambda b, r, i, h: (b, i, r * cb + c0 + h))

    def col_prev(c0):
        return pl.BlockSpec((None, BLOCK, hw), lambda b, r, i, h: (b, jnp.maximum(i - 1, 0), r * cb + c0 + h))

    tab = pl.BlockSpec((None, BLOCK, HEAD_DIM), lambda b, r, i, h: (r, i, 0))
    tab_prev = pl.BlockSpec((None, BLOCK, HEAD_DIM), lambda b, r, i, h: (r, jnp.maximum(i - 1, 0), 0))
    in_specs = [col(q0), col(k0), col(v0), tab, tab]
    args = [view, view, view, cos2, sin2]
    if has_prev:
        in_specs += [col_prev(k0), col_prev(v0), tab_prev, tab_prev]
        args += [view, view, cos2, sin2]
    out_col = pl.BlockSpec((None, BLOCK, hw), lambda b, r, i, h: (b, i, r * nhb + h))
    n_out, stats, k_rot = pl.pallas_call(
        functools.partial(_dilated_kernel, has_prev),
        grid=(bsz, dil, nb, nhb),
        in_specs=in_specs,
        out_specs=[out_col,
                   pl.BlockSpec((None, None, None, BLOCK, 2 * A_HB), lambda b, r, i, h: (b, r, h, i, 0)),
                   out_col],
        out_shape=[jax.ShapeDtypeStruct((bsz, n_sub, dil * gw), F32),
                   jax.ShapeDtypeStruct((bsz, dil, nhb, n_sub, 2 * A_HB), F32),
                   jax.ShapeDtypeStruct((bsz, n_sub, dil * gw), F32)],
        compiler_params=_params(4),
        name="dilated_prompt",
    )(*args)
    stats = stats.reshape(bsz, dil, nhb, n_sub, 2, A_HB).transpose(4, 0, 3, 1, 2, 5).reshape(2, bsz, s, n_heads)
    return n_out.reshape(bsz, s, gw), stats[0], stats[1], k_rot.reshape(bsz, s, gw)


def _merge_kernel(n1_ref, n2_ref, n3_ref, m_ref, l_ref, o_ref):
    ms, ls = m_ref[...], l_ref[...]
    top = jnp.maximum(jnp.maximum(ms[0], ms[1]), ms[2])
    ws = [jnp.exp(ms[g] - top) for g in range(3)]
    inv = 1.0 / (ws[0] * ls[0] + ws[1] * ls[1] + ws[2] * ls[2])
    n_heads = ms.shape[-1]
    for h in range(n_heads):
        sl = slice(h * HEAD_DIM, (h + 1) * HEAD_DIM)
        num = (ws[0][:, h:h + 1] * n1_ref[:, sl] + ws[1][:, h:h + 1] * n2_ref[:, sl]
               + ws[2][:, h:h + 1] * n3_ref[:, sl])
        o_ref[:, sl] = (num * inv[:, h:h + 1]).astype(BF16)


def _merge_groups(ns, ms, ls, *, tm):
    m, gw = ns[0].shape
    n_heads = gw // HEAD_DIM
    assert m % tm == 0
    row = pl.BlockSpec((tm, gw), lambda i: (i, 0))
    st = pl.BlockSpec((3, tm, n_heads), lambda i: (0, i, 0))
    return pl.pallas_call(
        _merge_kernel,
        grid=(m // tm,),
        in_specs=[row, row, row, st, st],
        out_specs=row,
        out_shape=jax.ShapeDtypeStruct((m, gw), BF16),
        compiler_params=_params(1),
        name="merge_groups",
    )(ns[0], ns[1], ns[2], jnp.stack(ms), jnp.stack(ls))


def _log_sigmoid_pair(z):
    t = jnp.log1p(jnp.exp(-jnp.abs(z)))
    return -(jnp.maximum(-z, 0.0) + t), -(jnp.maximum(z, 0.0) + t)


def _suffix_sum(x, tri):
    hi = x.astype(BF16)
    r1 = x - hi.astype(F32)
    mid = r1.astype(BF16)
    lo = (r1 - mid.astype(F32)).astype(BF16)
    out = jnp.dot(hi, tri, preferred_element_type=F32)
    out = out + jnp.dot(mid, tri, preferred_element_type=F32)
    return out + jnp.dot(lo, tri, preferred_element_type=F32)


def _suffix_tri(n):
    return jnp.where(lax.broadcasted_iota(I32, (n, n), 0) >= lax.broadcasted_iota(I32, (n, n), 1),
                     1.0, 0.0).astype(BF16)


SB_HB = 4


def _sb_prompt_kernel(bias_ref, q_ref, k_ref, v_ref, o_ref, acc_ref, run_ref):
    hb_idx = pl.program_id(1)
    qb = pl.program_id(2)
    tri = _suffix_tri(BLOCK)
    row = lax.broadcasted_iota(I32, (BLOCK, BLOCK), 0)
    colm = lax.broadcasted_iota(I32, (BLOCK, BLOCK), 1)
    acc_ref[...] = jnp.zeros(acc_ref.shape, F32)
    run_ref[...] = jnp.zeros(run_ref.shape, F32)

    def block(t, carry):
        off = pl.multiple_of((qb - t) * BLOCK, BLOCK)
        causal = (colm < row) | (t > 0)
        for h in range(SB_HB):
            sl = slice(h * HEAD_DIM, (h + 1) * HEAD_DIM)
            q = q_ref[:, sl].astype(BF16)
            k = k_ref[pl.ds(off, BLOCK), sl].astype(BF16)
            v = v_ref[pl.ds(off, BLOCK), sl].astype(BF16)
            z = lax.dot_general(q, k, (((1,), (1,)), ((), ())),
                                preferred_element_type=F32) * ATTN_SCALE + bias_ref[hb_idx * SB_HB + h]
            log_sig, log_keep = _log_sigmoid_pair(z)
            log_keep = jnp.where(causal, log_keep, 0.0)
            after = _suffix_sum(log_keep, tri)
            a = jnp.where(causal, jnp.exp(log_sig + (after - log_keep) + run_ref[h]), 0.0)
            acc_ref[:, sl] += jnp.dot(a.astype(BF16), v, preferred_element_type=F32)
            run_ref[h] += after[:, 0:1]
        return carry

    lax.fori_loop(0, qb + 1, block, 0)
    o_ref[...] = acc_ref[...].astype(BF16)


def _sb_prompt(qkv, bias, n_heads):
    bsz, s, _ = qkv.shape
    assert s % BLOCK == 0 and n_heads % SB_HB == 0
    hw = SB_HB * HEAD_DIM
    nhb = n_heads // SB_HB
    return pl.pallas_call(
        _sb_prompt_kernel,
        grid=(bsz, nhb, s // BLOCK),
        in_specs=[pl.BlockSpec(memory_space=pltpu.SMEM),
                  pl.BlockSpec((None, BLOCK, hw), lambda b, h, i: (b, i, h)),
                  pl.BlockSpec((None, s, hw), lambda b, h, i: (b, 0, nhb + h)),
                  pl.BlockSpec((None, s, hw), lambda b, h, i: (b, 0, 2 * nhb + h))],
        out_specs=pl.BlockSpec((None, BLOCK, hw), lambda b, h, i: (b, i, h)),
        out_shape=jax.ShapeDtypeStruct((bsz, s, n_heads * HEAD_DIM), BF16),
        scratch_shapes=[pltpu.VMEM((BLOCK, hw), F32), pltpu.VMEM((SB_HB, BLOCK, 1), F32)],
        compiler_params=_params(3),
        name="stick_breaking_prompt",
    )(bias, qkv, qkv, qkv)


def _rows8(row):
    return jnp.broadcast_to(row, (8, row.shape[1]))


def _sb_step_kernel(pt_ref, q_ref, bias_ref, k_ref, v_ref, o_ref, acc_ref, run_ref, z_ref, a_ref):
    del pt_ref
    j = pl.program_id(1)
    n_heads = q_ref.shape[0]

    @pl.when(j == 0)
    def _init():
        acc_ref[...] = jnp.zeros(acc_ref.shape, F32)
        run_ref[...] = jnp.zeros(run_ref.shape, F32)

    for h in range(n_heads):
        q8 = _rows8(q_ref[h:h + 1, :]).astype(BF16)
        zh = lax.dot_general(q8, k_ref[:, h, :].astype(BF16), (((1,), (1,)), ((), ())),
                             preferred_element_type=F32)
        z_ref[h:h + 1, :] = zh[0:1]
    z = z_ref[...] * ATTN_SCALE + bias_ref[...]
    log_sig, log_keep = _log_sigmoid_pair(z)
    after = _suffix_sum(log_keep, _suffix_tri(z.shape[1]))
    a_ref[...] = jnp.exp(log_sig + (after - log_keep) + run_ref[...])
    run_ref[...] += after[:, 0:1]
    for h in range(n_heads):
        a8 = _rows8(a_ref[h:h + 1, :]).astype(BF16)
        oh = jnp.dot(a8, v_ref[:, h, :].astype(BF16), preferred_element_type=F32)
        acc_ref[h:h + 1, :] += oh[0:1]

    @pl.when(j == pl.num_programs(1) - 1)
    def _done():
        o_ref[...] = acc_ref[...]


def _sb_step(q, bias, k_pool, v_pool, layer, page_table):
    bsz, n_heads, _ = q.shape
    n_pages = page_table.shape[1]
    page = k_pool.shape[2]
    pool = pl.BlockSpec((None, None, page, n_heads, HEAD_DIM),
                        lambda b, j, pt: (layer, pt[b, n_pages - 1 - j], 0, 0, 0))
    vec = pl.BlockSpec((None, n_heads, HEAD_DIM), lambda b, j, pt: (b, 0, 0))
    grid_spec = pltpu.PrefetchScalarGridSpec(
        num_scalar_prefetch=1,
        grid=(bsz, n_pages),
        in_specs=[vec, pl.BlockSpec((n_heads, 1), lambda b, j, pt: (0, 0)), pool, pool],
        out_specs=vec,
        scratch_shapes=[pltpu.VMEM((n_heads, HEAD_DIM), F32), pltpu.VMEM((n_heads, 1), F32),
                        pltpu.VMEM((n_heads, page), F32), pltpu.VMEM((n_heads, page), F32)],
    )
    return pl.pallas_call(
        _sb_step_kernel,
        grid_spec=grid_spec,
        out_shape=jax.ShapeDtypeStruct((bsz, n_heads, HEAD_DIM), F32),
        compiler_params=_params(2),
        name="stick_breaking_step",
    )(page_table, q, bias.reshape(n_heads, 1), k_pool, v_pool)


def _dilated_step_kernel(qkv_ref, cos_ref, sin_ref, k1_ref, v1_ref, k2_ref, v2_ref, k3_ref, v3_ref,
                         o_ref, kr_ref, s_ref, p_ref, r_ref):
    n_keys, n_heads, _ = k1_ref.shape
    cos2, sin2 = cos_ref[...], sin_ref[...]
    caches = ((k1_ref, v1_ref), (k2_ref, v2_ref), (k3_ref, v3_ref))
    lane = lax.broadcasted_iota(I32, (n_heads, n_keys + 8), 1)
    ms, ls = [], []
    for g, (kc_ref, vc_ref) in enumerate(caches):
        base = g * 3 * n_heads
        q = _rope(qkv_ref[base:base + n_heads, :], cos2, sin2)
        k_new = _rope(qkv_ref[base + n_heads:base + 2 * n_heads, :], cos2, sin2)
        v_new = qkv_ref[base + 2 * n_heads:base + 3 * n_heads, :]
        kr_ref[g * n_heads:(g + 1) * n_heads, :] = k_new
        for h in range(n_heads):
            k_all = jnp.concatenate([kc_ref[:, h, :], _rows8(k_new[h:h + 1])], axis=0).astype(BF16)
            sh = lax.dot_general(_rows8(q[h:h + 1]).astype(BF16), k_all, (((1,), (1,)), ((), ())),
                                 preferred_element_type=F32)
            s_ref[h:h + 1, :] = sh[0:1]
        s = jnp.where(lane <= n_keys, s_ref[...] * ATTN_SCALE, -jnp.inf)
        m = jnp.max(s, axis=-1, keepdims=True)
        p = jnp.exp(s - m)
        ms.append(m)
        ls.append(jnp.sum(p, axis=-1, keepdims=True))
        p_ref[...] = p
        for h in range(n_heads):
            v_all = jnp.concatenate([vc_ref[:, h, :], _rows8(v_new[h:h + 1])], axis=0).astype(BF16)
            rh = jnp.dot(_rows8(p_ref[h:h + 1, :]).astype(BF16), v_all, preferred_element_type=F32)
            r_ref[g, h:h + 1, :] = rh[0:1]
    top = jnp.maximum(jnp.maximum(ms[0], ms[1]), ms[2])
    ws = [jnp.exp(m - top) for m in ms]
    num = ws[0] * r_ref[0] + ws[1] * r_ref[1] + ws[2] * r_ref[2]
    den = ws[0] * ls[0] + ws[1] * ls[1] + ws[2] * ls[2]
    o_ref[...] = num / den


def _dilated_step(qkv, caches_k, caches_v, layer, past_len):
    bsz, rows, _ = qkv.shape
    n_heads = caches_k[0].shape[3]
    cos2, sin2 = _rope_tables(jnp.full((1,), past_len, I32))
    args, specs = [], []
    for kc, vc, dil in zip(caches_k, caches_v, A_DILATIONS):
        wb = kc.shape[2]
        assert wb == dil * A_KEYS, "cache must hold exactly the window"
        for cache in (kc, vc):
            args.append(cache.reshape(cache.shape[0], bsz, wb // dil, dil * n_heads, HEAD_DIM))
            specs.append(pl.BlockSpec((None, None, wb // dil, n_heads, HEAD_DIM),
                                      lambda b: (layer, b, 0, 0, 0)))
    slab = lambda r: pl.BlockSpec((None, r, HEAD_DIM), lambda b: (b, 0, 0))
    tab = pl.BlockSpec((1, HEAD_DIM), lambda b: (0, 0))
    return pl.pallas_call(
        _dilated_step_kernel,
        grid=(bsz,),
        in_specs=[slab(rows), tab, tab] + specs,
        out_specs=[slab(n_heads), slab(3 * n_heads)],
        out_shape=[jax.ShapeDtypeStruct((bsz, n_heads, HEAD_DIM), F32),
                   jax.ShapeDtypeStruct((bsz, 3 * n_heads, HEAD_DIM), F32)],
        scratch_shapes=[pltpu.VMEM((n_heads, A_KEYS + 8), F32), pltpu.VMEM((n_heads, A_KEYS + 8), F32),
                        pltpu.VMEM((3, n_heads, HEAD_DIM), F32)],
        compiler_params=_params(1),
        name="dilated_step",
    )(qkv, cos2, sin2, *args)


def _mem_attend_kernel(q_ref, k_ref, v_ref, o_ref):
    n_heads = q_ref.shape[-1] // HEAD_DIM
    for h in range(n_heads):
        sl = slice(h * HEAD_DIM, (h + 1) * HEAD_DIM)
        s = lax.dot_general(q_ref[:, sl].astype(BF16), k_ref[:, sl].astype(BF16), (((1,), (1,)), ((), ())),
                            preferred_element_type=F32) * ATTN_SCALE
        m = jnp.max(s, axis=-1, keepdims=True)
        e = jnp.exp(s - m)
        p = e / jnp.sum(e, axis=-1, keepdims=True)
        o = jnp.dot(p.astype(BF16), v_ref[:, sl].astype(BF16), preferred_element_type=F32)
        o_ref[:, sl] = o.astype(o_ref.dtype)


def _mem_step_kernel(q_ref, k_ref, v_ref, o_ref):
    for h in range(q_ref.shape[0]):
        s = lax.dot_general(_rows8(q_ref[h:h + 1, :]).astype(BF16), k_ref[:, h, :].astype(BF16),
                            (((1,), (1,)), ((), ())), preferred_element_type=F32) * ATTN_SCALE
        m = jnp.max(s, axis=-1, keepdims=True)
        e = jnp.exp(s - m)
        p = e / jnp.sum(e, axis=-1, keepdims=True)
        o = jnp.dot(p.astype(BF16), v_ref[:, h, :].astype(BF16), preferred_element_type=F32)
        o_ref[h:h + 1, :] = o[0:1]


def _mem_attend_prompt(q, kv, bsz, *, tq):
    rows, w = q.shape
    s = rows // bsz
    mem = kv.shape[0] // bsz
    assert s % tq == 0
    nq = s // tq
    return pl.pallas_call(
        _mem_attend_kernel,
        grid=(bsz, nq),
        in_specs=[pl.BlockSpec((tq, w), lambda b, i: (b * nq + i, 0)),
                  pl.BlockSpec((mem, w), lambda b, i: (b, 0)),
                  pl.BlockSpec((mem, w), lambda b, i: (b, 1))],
        out_specs=pl.BlockSpec((tq, w), lambda b, i: (b * nq + i, 0)),
        out_shape=jax.ShapeDtypeStruct((rows, w), BF16),
        compiler_params=_params(2),
        name="mem_attend_prompt",
    )(q, kv, kv)


def _mem_attend_step(q, k_cache, v_cache, layer):
    bsz, n_heads, _ = q.shape
    mem = k_cache.shape[2]
    vec = pl.BlockSpec((None, n_heads, HEAD_DIM), lambda b: (b, 0, 0))
    cache = pl.BlockSpec((None, None, mem, n_heads, HEAD_DIM), lambda b: (layer, b, 0, 0, 0))
    return pl.pallas_call(
        _mem_step_kernel,
        grid=(bsz,),
        in_specs=[vec, cache, cache],
        out_specs=vec,
        out_shape=jax.ShapeDtypeStruct((bsz, n_heads, HEAD_DIM), F32),
        compiler_params=_params(1),
        name="mem_attend_step",
    )(q, k_cache, v_cache)


PROMPT_TM = 1024
STEP_ROWS = 8


def _proj(x, w, layer, *, tn=256, n_cols=None, col0=0):
    tm = PROMPT_TM if x.shape[0] % PROMPT_TM == 0 else x.shape[0]
    return _matmul(x, w, layer, tm=tm, tn=tn, n_cols=n_cols, col0=col0)


def kernel(x_prompt, x_sample, cache_a1_k, cache_a1_v, cache_a2_k, cache_a2_v, cache_a3_k, cache_a3_v,
           cache_b_k, cache_b_v, cache_mem_k, cache_mem_v, page_table, mem_prompt,
           a_w_in, a_w_out, b_w_in, b_w_out, b_logit_bias, mem_w_q, mem_w_kv, mem_w_out, ln_g, ln_b,
           router_w, router_b, moe_w_gu, moe_b_gu, moe_w_down, moe_b_down):
    bp, sp, d = x_prompt.shape
    bs, ts, _ = x_sample.shape
    assert ts == 1 and bs == STEP_ROWS
    depth = ln_g.shape[0]
    alpha = (2 * depth) ** 0.25
    past_len = page_table.shape[1] * PAGE_SIZE
    a_heads = cache_a1_k.shape[3]
    a_gw = a_heads * HEAD_DIM
    b_heads = cache_b_k.shape[3]
    b_gw = b_heads * HEAD_DIM
    mem_len, mem_heads = cache_mem_k.shape[2], cache_mem_k.shape[3]
    mem_w = mem_heads * HEAD_DIM
    caches_k = (cache_a1_k, cache_a2_k, cache_a3_k)
    caches_v = (cache_a1_v, cache_a2_v, cache_a3_v)
    mem_rows = mem_prompt.reshape(bp * mem_len, d).astype(BF16)

    xp = x_prompt.reshape(bp * sp, d)
    xs = x_sample.reshape(bs, d)
    xpb, xsb = xp.astype(BF16), xs
    pa_k, pa_v, sa_k, sa_v = ([[] for _ in range(3)] for _ in range(4))
    pb_k, pb_v, sb_k, sb_v, pm_k, pm_v = [], [], [], [], [], []

    for i in range(depth):
        j = i // 2
        if i % 2 == 0:
            qkv_p = _proj(xpb, a_w_in, j).reshape(bp, sp, -1)
            qkv_s = _proj(xsb, a_w_in, j).reshape(bs, -1, HEAD_DIM)
            ns, ms, ls = [], [], []
            for g, (win, dil) in enumerate(zip(A_WINDOWS, A_DILATIONS)):
                n_g, m_g, l_g, k_rot = _dilated_prompt(qkv_p, g, dil, a_heads)
                ns.append(n_g.reshape(bp * sp, a_gw))
                ms.append(m_g.reshape(bp * sp, a_heads))
                ls.append(l_g.reshape(bp * sp, a_heads))
                keep = min(win, sp)
                v_g = qkv_p[:, :, (3 * g + 2) * a_gw:(3 * g + 3) * a_gw]
                pa_k[g].append(k_rot[:, sp - keep:].reshape(bp, keep, a_heads, HEAD_DIM))
                pa_v[g].append(v_g[:, sp - keep:].reshape(bp, keep, a_heads, HEAD_DIM))
            op = _merge_groups(ns, ms, ls, tm=256)
            os_, k_new = _dilated_step(qkv_s, caches_k, caches_v, j, past_len)
            for g in range(3):
                sa_k[g].append(k_new[:, g * a_heads:(g + 1) * a_heads].reshape(bs, 1, a_heads, HEAD_DIM))
                sa_v[g].append(qkv_s[:, (3 * g + 2) * a_heads:(3 * g + 3) * a_heads]
                               .reshape(bs, 1, a_heads, HEAD_DIM))
            fp = _proj(op, a_w_out, j)
            fs = _proj(os_.reshape(bs, a_gw), a_w_out, j)
        else:
            qkv_p = _proj(xpb, b_w_in, j).reshape(bp, sp, -1)
            qkv_s = _proj(xsb, b_w_in, j).reshape(bs, 3, b_heads, HEAD_DIM)
            op = _sb_prompt(qkv_p, b_logit_bias[j], b_heads).reshape(bp * sp, b_gw)
            os_ = _sb_step(qkv_s[:, 0], b_logit_bias[j], cache_b_k, cache_b_v, j, page_table)
            pb_k.append(qkv_p[:, :, b_gw:2 * b_gw].reshape(bp, sp, b_heads, HEAD_DIM))
            pb_v.append(qkv_p[:, :, 2 * b_gw:].reshape(bp, sp, b_heads, HEAD_DIM))
            sb_k.append(qkv_s[:, 1].reshape(bs, 1, b_heads, HEAD_DIM))
            sb_v.append(qkv_s[:, 2].reshape(bs, 1, b_heads, HEAD_DIM))
            fp = _proj(op, b_w_out, j)
            fs = _proj(os_.reshape(bs, b_gw), b_w_out, j)
        xp, xpb = _deepnorm(xp, fp, ln_g[i, 0], ln_b[i, 0], alpha, tm=256)
        xs, xsb = _deepnorm(xs, fs, ln_g[i, 0], ln_b[i, 0], alpha, tm=bs)

        kv = _proj(mem_rows, mem_w_kv, i)
        kv4 = kv.reshape(bp, mem_len, 2, mem_heads, HEAD_DIM)
        pm_k.append(kv4[:, :, 0])
        pm_v.append(kv4[:, :, 1])
        qp = _proj(xpb, mem_w_q, i)
        qs = _proj(xsb, mem_w_q, i)
        op = _mem_attend_prompt(qp, kv, bp, tq=512)
        os_ = _mem_attend_step(qs.reshape(bs, mem_heads, HEAD_DIM), cache_mem_k, cache_mem_v, i)
        fp = _proj(op, mem_w_out, i)
        fs = _proj(os_.reshape(bs, mem_w), mem_w_out, i)
        xp, xpb = _deepnorm(xp, fp, ln_g[i, 1], ln_b[i, 1], alpha, tm=256)
        xs, xsb = _deepnorm(xs, fs, ln_g[i, 1], ln_b[i, 1], alpha, tm=bs)

        xp, xpb, xs, xsb = _moe_layer(xp, xpb, xs, i, router_w, router_b, moe_w_gu, moe_b_gu,
                                      moe_w_down, moe_b_down, ln_g[i, 2], ln_b[i, 2], alpha)

    stack = jnp.stack
    return (xp.reshape(bp, sp, d), xs.reshape(bs, ts, d),
            stack(pa_k[0]), stack(pa_v[0]), stack(pa_k[1]), stack(pa_v[1]), stack(pa_k[2]), stack(pa_v[2]),
            stack(pb_k), stack(pb_v), stack(pm_k), stack(pm_v),
            stack(sa_k[0]), stack(sa_v[0]), stack(sa_k[1]), stack(sa_v[1]), stack(sa_k[2]), stack(sa_v[2]),
            stack(sb_k), stack(sb_v))
```

```python
import functools

import jax
import jax.numpy as jnp
import numpy as np
from jax import lax
from jax.experimental import pallas as pl
from jax.experimental.pallas import tpu as pltpu

F32 = jnp.float32
BF16 = jnp.bfloat16
I32 = jnp.int32

HEAD_DIM = 128
ATTN_SCALE = HEAD_DIM ** -0.5
TOP_K = 4
SWIGLU_LIMIT = 7.0
SWIGLU_ALPHA = 1.702
ROPE_THETA = 10000.0
LN_EPS = 1e-5
A_WINDOWS = (128, 512, 2048)
A_DILATIONS = (1, 4, 16)
A_KEYS = 128
BLOCK = 128
PAGE_SIZE = 128

V7X_VMEM_BYTES = 64 * 1024 * 1024
VMEM_LIMIT = (V7X_VMEM_BYTES * 3) // 4
MOSAIC_TEMP_BYTES = 8 * 1024 * 1024
MOE_TM = 256


def _params(n_axes, vmem_limit=VMEM_LIMIT):
    return pltpu.CompilerParams(dimension_semantics=("arbitrary",) * n_axes,
                                vmem_limit_bytes=vmem_limit)


def _block_bytes(*shape_dtypes):
    total = 0
    for shape, dtype in shape_dtypes:
        n = 1
        for s in shape:
            n *= s
        total += 2 * n * jnp.dtype(dtype).itemsize
    return total


def _matmul_kernel(x_ref, w_ref, o_ref, wb_ref):
    @pl.when(pl.program_id(1) == 0)
    def _cast():
        wb_ref[...] = w_ref[...].astype(BF16)

    o_ref[...] = jnp.dot(x_ref[...].astype(BF16), wb_ref[...], preferred_element_type=F32)


def _matmul(x, w, layer, *, tm, tn, n_cols=None, col0=0):
    m, k = x.shape
    n = w.shape[2] if n_cols is None else n_cols
    assert m % tm == 0 and n % tn == 0 and col0 % tn == 0
    cb = col0 // tn
    return pl.pallas_call(
        _matmul_kernel,
        grid=(n // tn, m // tm),
        in_specs=[pl.BlockSpec((tm, k), lambda j, i: (i, 0)),
                  pl.BlockSpec((None, k, tn), lambda j, i: (layer, 0, cb + j))],
        out_specs=pl.BlockSpec((tm, tn), lambda j, i: (i, j)),
        out_shape=jax.ShapeDtypeStruct((m, n), F32),
        scratch_shapes=[pltpu.VMEM((k, tn), BF16)],
        compiler_params=_params(2),
        name="matmul",
    )(x, w)


def _layer_norm_rows(y, g, b):
    mu = jnp.mean(y, axis=-1, keepdims=True)
    d = y - mu
    var = jnp.mean(d * d, axis=-1, keepdims=True)
    return d * lax.rsqrt(var + LN_EPS) * g + b


BF16_ROWS = 16


def _operand_dtype(rows):
    return BF16 if rows % BF16_ROWS == 0 else F32


U32 = jnp.uint32
HIGH_HALF = np.uint32(0xFFFF0000)


def _pack_bf16_pairs(y):
    half = y.shape[1] // 2
    lo = lax.bitcast_convert_type(y[:, :half].astype(BF16).astype(F32), U32) >> 16
    hi = lax.bitcast_convert_type(y[:, half:].astype(BF16).astype(F32), U32) & HIGH_HALF
    return hi | lo


def _unpack_bf16_pairs(w):
    return (lax.bitcast_convert_type(w << 16, F32), lax.bitcast_convert_type(w & HIGH_HALF, F32))


def _deepnorm_kernel(alpha, x_ref, f_ref, g_ref, b_ref, o_ref, ob_ref, *packed_ref):
    y = _layer_norm_rows(alpha * x_ref[...] + f_ref[...], g_ref[...], b_ref[...])
    o_ref[...] = y
    ob_ref[...] = y.astype(ob_ref.dtype)
    if packed_ref:
        packed_ref[0][...] = _pack_bf16_pairs(y)


def _deepnorm(x, f, g, b, alpha, *, tm, packed=False):
    m, d = x.shape
    assert m % tm == 0
    row = pl.BlockSpec((tm, d), lambda i: (i, 0))
    vec = pl.BlockSpec((1, d), lambda i: (0, 0))
    out_specs = [row, row]
    out_shape = [jax.ShapeDtypeStruct((m, d), F32), jax.ShapeDtypeStruct((m, d), _operand_dtype(m))]
    if packed:
        out_specs.append(pl.BlockSpec((tm, d // 2), lambda i: (i, 0)))
        out_shape.append(jax.ShapeDtypeStruct((m, d // 2), U32))
    return pl.pallas_call(
        functools.partial(_deepnorm_kernel, alpha),
        grid=(m // tm,),
        in_specs=[row, row, vec, vec],
        out_specs=out_specs,
        out_shape=out_shape,
        compiler_params=_params(1),
        name="deepnorm",
    )(x, f, g.reshape(1, d), b.reshape(1, d))


def _router_kernel(n_valid, x_ref, wrt_ref, br_ref, cin_ref,
                   idx_ref, gate_ref, rank_ref, cout_ref, carry_ref):
    i = pl.program_id(0)
    n_exp, tm = wrt_ref.shape[0], x_ref.shape[0]

    @pl.when(i == 0)
    def _init():
        carry_ref[...] = cin_ref[...]

    logits = lax.dot_general(wrt_ref[...], x_ref[...].astype(BF16), (((1,), (1,)), ((), ())),
                             preferred_element_type=F32) + br_ref[...]
    e_iota = lax.broadcasted_iota(I32, (n_exp, tm), 0)
    tok = i * tm + lax.broadcasted_iota(I32, (1, tm), 1)
    live = tok < n_valid
    work = logits
    vals, idxs, hots = [], [], []
    for _ in range(TOP_K):
        m = jnp.max(work, axis=0, keepdims=True)
        sel = jnp.min(jnp.where(work == m, e_iota, n_exp), axis=0, keepdims=True)
        hot = e_iota == sel
        vals.append(m)
        idxs.append(sel)
        hots.append(hot)
        work = jnp.where(hot, -jnp.inf, work)
    exps = [jnp.exp(v - vals[0]) for v in vals]
    denom = exps[0] + exps[1] + exps[2] + exps[3]
    any_hot = jnp.where((hots[0] | hots[1] | hots[2] | hots[3]) & live, 1.0, 0.0)
    earlier = (lax.broadcasted_iota(I32, (tm, tm), 0) < lax.broadcasted_iota(I32, (tm, tm), 1))
    before = jnp.dot(any_hot.astype(BF16), jnp.where(earlier, 1.0, 0.0).astype(BF16),
                     preferred_element_type=F32) + carry_ref[...]
    for k in range(TOP_K):
        idx_ref[k:k + 1, :] = idxs[k]
        gate_ref[k:k + 1, :] = exps[k] / denom
        rank_ref[k:k + 1, :] = jnp.sum(jnp.where(hots[k], before, 0.0), axis=0,
                                       keepdims=True).astype(I32)
    carry_ref[...] = carry_ref[...] + jnp.sum(any_hot, axis=1, keepdims=True)
    cout_ref[...] = carry_ref[...]


def _router(x, wrt, br, counts_in, n_valid, *, tm):
    m, d = x.shape
    n_exp = wrt.shape[0]
    assert m % tm == 0
    full = lambda shape: pl.BlockSpec(shape, lambda i: (0, 0))
    out_row = pl.BlockSpec((TOP_K, tm), lambda i: (0, i))
    return pl.pallas_call(
        functools.partial(_router_kernel, n_valid),
        grid=(m // tm,),
        in_specs=[pl.BlockSpec((tm, d), lambda i: (i, 0)), full((n_exp, d)), full((n_exp, 1)),
                  full((n_exp, 1))],
        out_specs=[out_row, out_row, out_row, full((n_exp, 1))],
        out_shape=[jax.ShapeDtypeStruct((TOP_K, m), I32), jax.ShapeDtypeStruct((TOP_K, m), F32),
                   jax.ShapeDtypeStruct((TOP_K, m), I32), jax.ShapeDtypeStruct((n_exp, 1), F32)],
        scratch_shapes=[pltpu.VMEM((n_exp, 1), F32)],
        compiler_params=_params(1),
        name="router",
    )(x, wrt, br, counts_in)


def _dispatch_kernel(pos_ref, x_ref, xs_in_ref, xs_ref, sem):
    del xs_in_ref
    tm = x_ref.shape[0]

    def row_copy(r, k):
        p = pos_ref[0, 0, k * tm + r]
        return pltpu.make_async_copy(x_ref.at[pl.ds(r, 1)], xs_ref.at[pl.ds(p, 1)], sem)

    def issue(r, carry):
        for k in range(TOP_K):
            row_copy(r, k).start()
        return carry

    lax.fori_loop(0, tm, issue, 0)

    def drain(r, carry):
        for k in range(TOP_K):
            row_copy(r, k).wait()
        return carry

    lax.fori_loop(0, tm, drain, 0)


def _dispatch(x, pos, xs, *, tm):
    m, d = x.shape
    assert m % tm == 0
    nt = m // tm
    pos_t = pos.reshape(TOP_K, nt, tm).transpose(1, 0, 2).reshape(nt, 1, TOP_K * tm)
    return pl.pallas_call(
        _dispatch_kernel,
        grid=(nt,),
        in_specs=[pl.BlockSpec((1, 1, TOP_K * tm), lambda i: (i, 0, 0), memory_space=pltpu.SMEM),
                  pl.BlockSpec((tm, d), lambda i: (i, 0)),
                  pl.BlockSpec(memory_space=pl.ANY)],
        out_specs=pl.BlockSpec(memory_space=pl.ANY),
        out_shape=jax.ShapeDtypeStruct(xs.shape, xs.dtype),
        scratch_shapes=[pltpu.SemaphoreType.DMA(())],
        input_output_aliases={2: 0},
        compiler_params=_params(1),
        name="moe_dispatch",
    )(pos_t, x, xs)


def _tile_is_first(te_ref, i):
    return jnp.logical_or(i == 0, te_ref[i] != te_ref[jnp.maximum(i - 1, 0)])


def _gmm1_kernel(te_ref, tr_ref, x_ref, wg_ref, wl_ref, bg_ref, bl_ref, h_ref, wgb_ref, wlb_ref):
    i = pl.program_id(1)

    @pl.when(_tile_is_first(te_ref, i))
    def _cast():
        wgb_ref[...] = wg_ref[...].astype(BF16)
        wlb_ref[...] = wl_ref[...].astype(BF16)

    rows = tr_ref[i]

    @pl.when(rows > 0)
    def _compute():
        tm, half = x_ref.shape
        keep = lax.broadcasted_iota(I32, (tm, 1), 0) < rows
        x_lo, x_hi = _unpack_bf16_pairs(x_ref[...])
        x_lo = jnp.where(keep, x_lo, 0.0).astype(BF16)
        x_hi = jnp.where(keep, x_hi, 0.0).astype(BF16)

        def project(w_ref, b_ref):
            return (jnp.dot(x_lo, w_ref[0:half, :], preferred_element_type=F32)
                    + jnp.dot(x_hi, w_ref[half:2 * half, :], preferred_element_type=F32) + b_ref[...])

        gate = project(wgb_ref, bg_ref)
        lin = project(wlb_ref, bl_ref)
        glu = jnp.minimum(gate, SWIGLU_LIMIT)
        lin = jnp.clip(lin, -SWIGLU_LIMIT, SWIGLU_LIMIT)
        h = glu * jax.nn.sigmoid(SWIGLU_ALPHA * glu) * (lin + 1.0)
        h_ref[...] = h.astype(BF16)

    @pl.when(rows <= 0)
    def _blank():
        h_ref[...] = jnp.zeros(h_ref.shape, BF16)


def _gmm1(xs, w_gu, b_gu, layer, tile_expert, tile_rows, *, tn):
    rows, half = xs.shape
    d = 2 * half
    ff = w_gu.shape[3] // 2
    n_exp = w_gu.shape[1]
    nt = rows // MOE_TM
    assert ff % tn == 0
    nc = ff // tn
    b4 = b_gu.reshape(b_gu.shape[0], n_exp, 1, 2 * ff)
    grid_spec = pltpu.PrefetchScalarGridSpec(
        num_scalar_prefetch=2,
        grid=(nc, nt),
        in_specs=[
            pl.BlockSpec((MOE_TM, half), lambda j, i, te, tr: (i, 0)),
            pl.BlockSpec((None, None, d, tn), lambda j, i, te, tr: (layer, te[i], 0, j)),
            pl.BlockSpec((None, None, d, tn), lambda j, i, te, tr: (layer, te[i], 0, nc + j)),
            pl.BlockSpec((None, None, 1, tn), lambda j, i, te, tr: (layer, te[i], 0, j)),
            pl.BlockSpec((None, None, 1, tn), lambda j, i, te, tr: (layer, te[i], 0, nc + j)),
        ],
        out_specs=pl.BlockSpec((MOE_TM, tn), lambda j, i, te, tr: (i, j)),
        scratch_shapes=[pltpu.VMEM((d, tn), BF16), pltpu.VMEM((d, tn), BF16)],
    )
    return pl.pallas_call(
        _gmm1_kernel,
        grid_spec=grid_spec,
        out_shape=jax.ShapeDtypeStruct((rows, ff), BF16),
        compiler_params=_params(2, _block_bytes(((MOE_TM, half), U32), ((d, tn), F32), ((d, tn), F32),
                                                ((d, tn), BF16), ((MOE_TM, tn), BF16))
                                + MOSAIC_TEMP_BYTES),
        name="moe_gmm1",
    )(tile_expert, tile_rows, xs, w_gu, w_gu, b4, b4)


def _gmm2_kernel(te_ref, tr_ref, h_ref, w_ref, b_ref, y_ref, wb_ref):
    i = pl.program_id(1)

    @pl.when(_tile_is_first(te_ref, i))
    def _cast():
        wb_ref[...] = w_ref[...].astype(BF16)

    @pl.when(tr_ref[i] > 0)
    def _compute():
        y_ref[...] = jnp.dot(h_ref[...], wb_ref[...], preferred_element_type=F32) + b_ref[...]

    @pl.when(tr_ref[i] <= 0)
    def _blank():
        y_ref[...] = jnp.zeros(y_ref.shape, F32)


def _gmm2(h, w_d, b_d, layer, tile_expert, tile_rows, *, tn):
    rows, ff = h.shape
    n_exp, d = w_d.shape[1], w_d.shape[3]
    nt = rows // MOE_TM
    assert d % tn == 0
    b4 = b_d.reshape(b_d.shape[0], n_exp, 1, d)
    grid_spec = pltpu.PrefetchScalarGridSpec(
        num_scalar_prefetch=2,
        grid=(d // tn, nt),
        in_specs=[
            pl.BlockSpec((MOE_TM, ff), lambda j, i, te, tr: (i, 0)),
            pl.BlockSpec((None, None, ff, tn), lambda j, i, te, tr: (layer, te[i], 0, j)),
            pl.BlockSpec((None, None, 1, tn), lambda j, i, te, tr: (layer, te[i], 0, j)),
        ],
        out_specs=pl.BlockSpec((MOE_TM, tn), lambda j, i, te, tr: (i, j)),
        scratch_shapes=[pltpu.VMEM((ff, tn), BF16)],
    )
    return pl.pallas_call(
        _gmm2_kernel,
        grid_spec=grid_spec,
        out_shape=jax.ShapeDtypeStruct((rows, d), F32),
        compiler_params=_params(2),
        name="moe_gmm2",
    )(tile_expert, tile_rows, h, w_d, b4)


def _combine_kernel(alpha, pos_ref, x_ref, gate_ref, g_ref, b_ref, y_ref, o_ref, ob_ref, buf_ref, sem):
    tm = x_ref.shape[0]

    def row_copy(r, k):
        p = pos_ref[0, 0, k * tm + r]
        return pltpu.make_async_copy(y_ref.at[pl.ds(p, 1)], buf_ref.at[k, pl.ds(r, 1)], sem)

    def issue(r, carry):
        for k in range(TOP_K):
            row_copy(r, k).start()
        return carry

    lax.fori_loop(0, tm, issue, 0)

    def drain(r, carry):
        for k in range(TOP_K):
            row_copy(r, k).wait()
        return carry

    lax.fori_loop(0, tm, drain, 0)

    gates = gate_ref[...]
    f = gates[:, 0:1] * buf_ref[0]
    for k in range(1, TOP_K):
        f = f + gates[:, k:k + 1] * buf_ref[k]
    y = _layer_norm_rows(alpha * x_ref[...] + f, g_ref[...], b_ref[...])
    o_ref[...] = y
    ob_ref[...] = y.astype(ob_ref.dtype)


def _combine(x, gates, pos, ys, g, b, alpha, *, tm):
    m, d = x.shape
    assert m % tm == 0
    nt = m // tm
    pos_t = pos.reshape(TOP_K, nt, tm).transpose(1, 0, 2).reshape(nt, 1, TOP_K * tm)
    row = pl.BlockSpec((tm, d), lambda i: (i, 0))
    vec = pl.BlockSpec((1, d), lambda i: (0, 0))
    return pl.pallas_call(
        functools.partial(_combine_kernel, alpha),
        grid=(nt,),
        in_specs=[pl.BlockSpec((1, 1, TOP_K * tm), lambda i: (i, 0, 0), memory_space=pltpu.SMEM),
                  row, pl.BlockSpec((tm, TOP_K), lambda i: (i, 0)), vec, vec,
                  pl.BlockSpec(memory_space=pl.ANY)],
        out_specs=[row, row],
        out_shape=[jax.ShapeDtypeStruct((m, d), F32), jax.ShapeDtypeStruct((m, d), _operand_dtype(m))],
        scratch_shapes=[pltpu.VMEM((TOP_K, tm, d), F32), pltpu.SemaphoreType.DMA(())],
        compiler_params=_params(1),
        name="moe_combine",
    )(pos_t, x, gates.T, g.reshape(1, d), b.reshape(1, d), ys)


def _moe_sorted_rows(n_tokens, n_exp):
    return ((n_tokens * TOP_K) // MOE_TM + n_exp) * MOE_TM


def _moe_layer(xp, xpb, xp_packed, xs, xs_packed, sorted_rows, layer, router_w, router_b,
               w_gu, b_gu, w_d, b_d, g, b, alpha):
    mp, d = xp.shape
    ms = xs.shape[0]
    n_exp = router_w.shape[2]
    wrt = router_w[layer].T.astype(BF16)
    br = router_b[layer].reshape(n_exp, 1)
    pad = 128
    xs_pad = jnp.zeros((pad, d), F32).at[:ms].set(xs)
    idx_p, gate_p, rank_p, cnt_p = _router(xpb, wrt, br, jnp.zeros((n_exp, 1), F32), mp, tm=512)
    idx_s, gate_s, rank_s, cnt = _router(xs_pad, wrt, br, cnt_p, ms, tm=pad)
    idx = jnp.concatenate([idx_p, idx_s[:, :ms]], axis=1)
    rank = jnp.concatenate([rank_p, rank_s[:, :ms]], axis=1)

    counts = cnt[:, 0].astype(I32)
    tiles = (counts + MOE_TM - 1) // MOE_TM
    tile_end = jnp.cumsum(tiles)
    tile_start = tile_end - tiles
    experts = jnp.arange(n_exp, dtype=I32)
    row_start = jnp.sum(jnp.where(idx[:, :, None] == experts, tile_start * MOE_TM, 0), axis=-1)
    pos = row_start + rank
    nt = sorted_rows.shape[0] // MOE_TM
    t_iota = jnp.arange(nt, dtype=I32)
    n_used = tile_end[-1]
    t_eff = jnp.minimum(t_iota, jnp.maximum(n_used - 1, 0))
    tile_expert = jnp.minimum(jnp.sum((tile_end[None, :] <= t_eff[:, None]).astype(I32), axis=1), n_exp - 1)
    hot = tile_expert[:, None] == experts
    tile_rows = (jnp.sum(jnp.where(hot, counts, 0), axis=1)
                 - (t_iota - jnp.sum(jnp.where(hot, tile_start, 0), axis=1)) * MOE_TM)
    tile_rows = jnp.where(t_iota < n_used, jnp.clip(tile_rows, 0, MOE_TM), 0).astype(I32)

    sorted_rows = _dispatch(xp_packed, pos[:, :mp], sorted_rows, tm=256)
    sorted_rows = _dispatch(xs_packed, pos[:, mp:], sorted_rows, tm=ms)
    h = _gmm1(sorted_rows, w_gu, b_gu, layer, tile_expert, tile_rows, tn=min(512, w_d.shape[2]))
    ys = _gmm2(h, w_d, b_d, layer, tile_expert, tile_rows, tn=min(2048, d))
    yp, ypb = _combine(xp, gate_p, pos[:, :mp], ys, g, b, alpha, tm=64)
    ysn, ysb = _combine(xs, gate_s[:, :ms], pos[:, mp:], ys, g, b, alpha, tm=ms)
    return yp, ypb, ysn, ysb, sorted_rows


def _rope_tables(pos):
    half = HEAD_DIM // 2
    inv_freq = ROPE_THETA ** (-jnp.arange(half, dtype=F32) / half)
    ang = pos.astype(F32)[..., None] * inv_freq
    cos, sin = jnp.cos(ang), jnp.sin(ang)
    return jnp.concatenate([cos, cos], axis=-1), jnp.concatenate([-sin, sin], axis=-1)


def _rope(x, cos2, sin2):
    width = x.shape[-1]
    half = HEAD_DIM // 2
    axis = x.ndim - 1
    if width == HEAD_DIM:
        return x * cos2 + pltpu.roll(x, half, axis=axis) * sin2
    first = lax.broadcasted_iota(I32, x.shape, axis) % HEAD_DIM < half
    partner = jnp.where(first, pltpu.roll(x, width - half, axis=axis), pltpu.roll(x, half, axis=axis))
    return x * cos2 + partner * sin2


A_HB = 4


def _dilated_kernel(has_prev, *refs):
    if has_prev:
        (q_ref, kc_ref, vc_ref, cc_ref, sc_ref, kp_ref, vp_ref, cp_ref, sp_ref,
         n_ref, st_ref, kr_ref) = refs
    else:
        q_ref, kc_ref, vc_ref, cc_ref, sc_ref, n_ref, st_ref, kr_ref = refs
    ib = pl.program_id(2)
    kw = 2 * BLOCK if has_prev else BLOCK
    qi = lax.broadcasted_iota(I32, (BLOCK, kw), 0)
    kc = lax.broadcasted_iota(I32, (BLOCK, kw), 1)
    dist = (kw - BLOCK) + qi - kc
    valid = (dist >= 0) & (dist <= A_KEYS)
    if has_prev:
        valid = valid & ((ib > 0) | (kc >= BLOCK))
    cos_c, sin_c = cc_ref[...], sc_ref[...]
    lane = lax.broadcasted_iota(I32, (BLOCK, 2 * A_HB), 1)
    stats = jnp.zeros((BLOCK, 2 * A_HB), F32)
    for h in range(A_HB):
        sl = slice(h * HEAD_DIM, (h + 1) * HEAD_DIM)
        q = _rope(q_ref[:, sl], cos_c, sin_c).astype(BF16)
        k_cur = _rope(kc_ref[:, sl], cos_c, sin_c)
        kr_ref[:, sl] = k_cur
        if has_prev:
            k_prev = _rope(kp_ref[:, sl], cp_ref[...], sp_ref[...])
            k_all = jnp.concatenate([k_prev, k_cur], axis=0).astype(BF16)
            v_all = jnp.concatenate([vp_ref[:, sl], vc_ref[:, sl]], axis=0).astype(BF16)
        else:
            k_all = k_cur.astype(BF16)
            v_all = vc_ref[:, sl].astype(BF16)
        s = lax.dot_general(q, k_all, (((1,), (1,)), ((), ())), preferred_element_type=F32) * ATTN_SCALE
        s = jnp.where(valid, s, -jnp.inf)
        m = jnp.max(s, axis=-1, keepdims=True)
        p = jnp.exp(s - m)
        l = jnp.sum(p, axis=-1, keepdims=True)
        n_ref[:, sl] = jnp.dot(p.astype(BF16), v_all, preferred_element_type=F32)
        stats = jnp.where(lane == h, m, stats)
        stats = jnp.where(lane == A_HB + h, l, stats)
    st_ref[...] = stats


def _dilated_prompt(qkv, group, dil, n_heads):
    bsz, s, c = qkv.shape
    n_sub = s // dil
    assert s % dil == 0 and n_sub % BLOCK == 0 and n_heads % A_HB == 0
    nb = n_sub // BLOCK
    has_prev = nb > 1
    hw = A_HB * HEAD_DIM
    gw = n_heads * HEAD_DIM
    nhb = n_heads // A_HB
    cb = c // hw
    q0, k0, v0 = (group * 3 * gw) // hw, ((group * 3 + 1) * gw) // hw, ((group * 3 + 2) * gw) // hw
    view = qkv.reshape(bsz, n_sub, dil * c)
    pos = (jnp.arange(n_sub, dtype=I32)[None, :] * dil + jnp.arange(dil, dtype=I32)[:, None])
    cos2, sin2 = _rope_tables(pos)

    def col(c0):
        return pl.BlockSpec((None, BLOCK, hw), lambda b, r, i, h: (b, i, r * cb + c0 + h))

    def col_prev(c0):
        return pl.BlockSpec((None, BLOCK, hw), lambda b, r, i, h: (b, jnp.maximum(i - 1, 0), r * cb + c0 + h))

    tab = pl.BlockSpec((None, BLOCK, HEAD_DIM), lambda b, r, i, h: (r, i, 0))
    tab_prev = pl.BlockSpec((None, BLOCK, HEAD_DIM), lambda b, r, i, h: (r, jnp.maximum(i - 1, 0), 0))
    in_specs = [col(q0), col(k0), col(v0), tab, tab]
    args = [view, view, view, cos2, sin2]
    if has_prev:
        in_specs += [col_prev(k0), col_prev(v0), tab_prev, tab_prev]
        args += [view, view, cos2, sin2]
    out_col = pl.BlockSpec((None, BLOCK, hw), lambda b, r, i, h: (b, i, r * nhb + h))
    n_out, stats, k_rot = pl.pallas_call(
        functools.partial(_dilated_kernel, has_prev),
        grid=(bsz, dil, nb, nhb),
        in_specs=in_specs,
        out_specs=[out_col,
                   pl.BlockSpec((None, None, None, BLOCK, 2 * A_HB), lambda b, r, i, h: (b, r, h, i, 0)),
                   out_col],
        out_shape=[jax.ShapeDtypeStruct((bsz, n_sub, dil * gw), F32),
                   jax.ShapeDtypeStruct((bsz, dil, nhb, n_sub, 2 * A_HB), F32),
                   jax.ShapeDtypeStruct((bsz, n_sub, dil * gw), F32)],
        compiler_params=_params(4),
        name="dilated_prompt",
    )(*args)
    stats = stats.reshape(bsz, dil, nhb, n_sub, 2, A_HB).transpose(4, 0, 3, 1, 2, 5).reshape(2, bsz, s, n_heads)
    return n_out.reshape(bsz, s, gw), stats[0], stats[1], k_rot.reshape(bsz, s, gw)


def _merge_kernel(n1_ref, n2_ref, n3_ref, m_ref, l_ref, o_ref):
    ms, ls = m_ref[...], l_ref[...]
    top = jnp.maximum(jnp.maximum(ms[0], ms[1]), ms[2])
    ws = [jnp.exp(ms[g] - top) for g in range(3)]
    inv = 1.0 / (ws[0] * ls[0] + ws[1] * ls[1] + ws[2] * ls[2])
    n_heads = ms.shape[-1]
    for h in range(n_heads):
        sl = slice(h * HEAD_DIM, (h + 1) * HEAD_DIM)
        num = (ws[0][:, h:h + 1] * n1_ref[:, sl] + ws[1][:, h:h + 1] * n2_ref[:, sl]
               + ws[2][:, h:h + 1] * n3_ref[:, sl])
        o_ref[:, sl] = (num * inv[:, h:h + 1]).astype(BF16)


def _merge_groups(ns, ms, ls, *, tm):
    m, gw = ns[0].shape
    n_heads = gw // HEAD_DIM
    assert m % tm == 0
    row = pl.BlockSpec((tm, gw), lambda i: (i, 0))
    st = pl.BlockSpec((3, tm, n_heads), lambda i: (0, i, 0))
    return pl.pallas_call(
        _merge_kernel,
        grid=(m // tm,),
        in_specs=[row, row, row, st, st],
        out_specs=row,
        out_shape=jax.ShapeDtypeStruct((m, gw), BF16),
        compiler_params=_params(1),
        name="merge_groups",
    )(ns[0], ns[1], ns[2], jnp.stack(ms), jnp.stack(ls))


def _log_keep(z):
    return -(jnp.maximum(z, 0.0) + jnp.log1p(jnp.exp(-jnp.abs(z))))


def _split2(x):
    hi = x.astype(BF16)
    return hi, (x - hi.astype(F32)).astype(BF16)


def _dot_split(x, w):
    hi, lo = _split2(x)
    return jnp.dot(hi, w, preferred_element_type=F32) + jnp.dot(lo, w, preferred_element_type=F32)


SB_HB = 4
SB_TQ = 512


def _sb_prompt_kernel(bias_ref, q_ref, k_ref, v_ref, o_ref, kb_ref, vb_ref, acc_ref, run_ref):
    hb_idx = pl.program_id(1)
    qi = pl.program_id(2)
    n_sub = SB_TQ // BLOCK

    @pl.when(qi == 0)
    def _cast():
        kb_ref[...] = k_ref[...].astype(BF16)
        vb_ref[...] = v_ref[...].astype(BF16)

    r = lax.broadcasted_iota(I32, (BLOCK, 2 * BLOCK), 0)
    c = lax.broadcasted_iota(I32, (BLOCK, 2 * BLOCK), 1)
    tri_ext = jnp.where((r >= c) | (c >= BLOCK), 1.0, 0.0).astype(BF16)
    acc_ref[...] = jnp.zeros(acc_ref.shape, F32)
    run_ref[...] = jnp.zeros(run_ref.shape, F32)

    def key_block(off, r0, mask):
        for h in range(SB_HB):
            sl = slice(h * HEAD_DIM, (h + 1) * HEAD_DIM)
            q = q_ref[r0:, sl].astype(BF16)
            k = kb_ref[pl.ds(off, BLOCK), sl]
            v = vb_ref[pl.ds(off, BLOCK), sl]
            z = lax.dot_general(q, k, (((1,), (1,)), ((), ())),
                                preferred_element_type=F32) * ATTN_SCALE + bias_ref[hb_idx * SB_HB + h]
            lk = _log_keep(z)
            if mask is not None:
                lk = jnp.where(mask, lk, 0.0)
            ext = _dot_split(lk, tri_ext)
            a = jnp.exp(z + ext[:, :BLOCK] + run_ref[h, r0:, :])
            if mask is not None:
                a = jnp.where(mask, a, 0.0)
            acc_ref[h, r0:, :] += jnp.dot(a.astype(BF16), v, preferred_element_type=F32)
            run_ref[h, r0:, :] += ext[:, BLOCK:]

    for cb in reversed(range(n_sub)):
        r0 = cb * BLOCK
        rows = SB_TQ - r0
        mask = (lax.broadcasted_iota(I32, (rows, BLOCK), 1) < lax.broadcasted_iota(I32, (rows, BLOCK), 0))
        key_block(pl.multiple_of((qi * n_sub + cb) * BLOCK, BLOCK), r0, mask)

    def earlier(t, carry):
        key_block(pl.multiple_of((qi * n_sub - 1 - t) * BLOCK, BLOCK), 0, None)
        return carry

    lax.fori_loop(0, qi * n_sub, earlier, 0)
    for h in range(SB_HB):
        o_ref[:, h * HEAD_DIM:(h + 1) * HEAD_DIM] = acc_ref[h].astype(BF16)


def _sb_prompt(qkv, bias, n_heads):
    bsz, s, _ = qkv.shape
    assert s % SB_TQ == 0 and n_heads % SB_HB == 0
    hw = SB_HB * HEAD_DIM
    nhb = n_heads // SB_HB
    return pl.pallas_call(
        _sb_prompt_kernel,
        grid=(bsz, nhb, s // SB_TQ),
        in_specs=[pl.BlockSpec(memory_space=pltpu.SMEM),
                  pl.BlockSpec((None, SB_TQ, hw), lambda b, h, i: (b, i, h)),
                  pl.BlockSpec((None, s, hw), lambda b, h, i: (b, 0, nhb + h)),
                  pl.BlockSpec((None, s, hw), lambda b, h, i: (b, 0, 2 * nhb + h))],
        out_specs=pl.BlockSpec((None, SB_TQ, hw), lambda b, h, i: (b, i, h)),
        out_shape=jax.ShapeDtypeStruct((bsz, s, n_heads * HEAD_DIM), BF16),
        scratch_shapes=[pltpu.VMEM((s, hw), BF16), pltpu.VMEM((s, hw), BF16),
                        pltpu.VMEM((SB_HB, SB_TQ, BLOCK), F32), pltpu.VMEM((SB_HB, SB_TQ, BLOCK), F32)],
        compiler_params=_params(3),
        name="stick_breaking_prompt",
    )(bias, qkv, qkv, qkv)


def _rows8(row):
    return jnp.broadcast_to(row, (8, row.shape[1]))


def _sb_step_kernel(pt_ref, q_ref, bias_ref, k_ref, v_ref, o_ref, acc_ref, run_ref, z_ref):
    del pt_ref
    j = pl.program_id(1)
    n_heads = q_ref.shape[0]
    n_rows = k_ref.shape[0] // 128

    @pl.when(j == 0)
    def _init():
        acc_ref[...] = jnp.zeros(acc_ref.shape, F32)
        run_ref[...] = jnp.zeros(run_ref.shape, F32)

    own = (lax.broadcasted_iota(I32, (n_heads, 128), 1) % n_heads
           == lax.broadcasted_iota(I32, (n_heads, 128), 0))
    s_all = lax.dot_general(q_ref[...].astype(BF16), k_ref[...].astype(BF16), (((1,), (1,)), ((), ())),
                            preferred_element_type=F32)
    for c in range(n_rows):
        z_ref[c:c + 1, :] = jnp.sum(jnp.where(own, s_all[:, c * 128:(c + 1) * 128], 0.0),
                                    axis=0, keepdims=True)
    z = z_ref[...] * ATTN_SCALE + bias_ref[...]
    lk = _log_keep(z)

    li = lax.broadcasted_iota(I32, (128, 256), 0)
    lj = lax.broadcasted_iota(I32, (128, 256), 1)
    same_head = li % n_heads == lj % n_heads
    w_ext = jnp.where(same_head & ((li // n_heads >= lj // n_heads) | (lj >= 128)), 1.0, 0.0).astype(BF16)
    ext = _dot_split(lk, w_ext)
    tot = ext[:, 128:]
    later_rows = jnp.where(lax.broadcasted_iota(I32, (n_rows, n_rows), 1)
                           > lax.broadcasted_iota(I32, (n_rows, n_rows), 0), 1.0, 0.0).astype(BF16)
    tot_hi, tot_lo = _split2(tot)
    after = (ext[:, :128] + jnp.dot(later_rows, tot_hi, preferred_element_type=F32)
             + jnp.dot(later_rows, tot_lo, preferred_element_type=F32))
    a = jnp.exp(z + after + run_ref[...])
    run_ref[...] += jnp.sum(tot, axis=0, keepdims=True)

    a_exp = jnp.concatenate(
        [jnp.where(own, jnp.broadcast_to(a[c:c + 1, :], (n_heads, 128)), 0.0) for c in range(n_rows)],
        axis=1).astype(BF16)
    acc_ref[...] += jnp.dot(a_exp, v_ref[...].astype(BF16), preferred_element_type=F32)

    @pl.when(j == pl.num_programs(1) - 1)
    def _done():
        o_ref[...] = acc_ref[...]


def _sb_step(q, bias, k_pool, v_pool, layer, page_table):
    bsz, n_heads, _ = q.shape
    n_pages = page_table.shape[1]
    n_layers, n_pool, page = k_pool.shape[:3]
    assert 128 % n_heads == 0 and (page * n_heads) % 128 == 0
    rows = page * n_heads
    pool = pl.BlockSpec((None, None, rows, HEAD_DIM),
                        lambda b, j, pt: (layer, pt[b, n_pages - 1 - j], 0, 0))
    vec = pl.BlockSpec((None, n_heads, HEAD_DIM), lambda b, j, pt: (b, 0, 0))
    grid_spec = pltpu.PrefetchScalarGridSpec(
        num_scalar_prefetch=1,
        grid=(bsz, n_pages),
        in_specs=[vec, pl.BlockSpec((1, 128), lambda b, j, pt: (0, 0)), pool, pool],
        out_specs=vec,
        scratch_shapes=[pltpu.VMEM((n_heads, HEAD_DIM), F32), pltpu.VMEM((1, 128), F32),
                        pltpu.VMEM((rows // 128, 128), F32)],
    )
    return pl.pallas_call(
        _sb_step_kernel,
        grid_spec=grid_spec,
        out_shape=jax.ShapeDtypeStruct((bsz, n_heads, HEAD_DIM), F32),
        compiler_params=_params(2),
        name="stick_breaking_step",
    )(page_table, q, jnp.tile(bias, 128 // n_heads).reshape(1, 128),
      k_pool.reshape(n_layers, n_pool, rows, HEAD_DIM), v_pool.reshape(n_layers, n_pool, rows, HEAD_DIM))


def _dilated_step_kernel(qkv_ref, cos_ref, sin_ref, k1_ref, v1_ref, k2_ref, v2_ref, k3_ref, v3_ref,
                         o_ref, kr_ref, s_ref, p_ref, r_ref):
    n_keys, n_heads, _ = k1_ref.shape
    cos2, sin2 = cos_ref[...], sin_ref[...]
    caches = ((k1_ref, v1_ref), (k2_ref, v2_ref), (k3_ref, v3_ref))
    lane = lax.broadcasted_iota(I32, (n_heads, n_keys + 8), 1)
    ms, ls = [], []
    for g, (kc_ref, vc_ref) in enumerate(caches):
        base = g * 3 * n_heads
        q = _rope(qkv_ref[base:base + n_heads, :], cos2, sin2)
        k_new = _rope(qkv_ref[base + n_heads:base + 2 * n_heads, :], cos2, sin2)
        v_new = qkv_ref[base + 2 * n_heads:base + 3 * n_heads, :]
        kr_ref[g * n_heads:(g + 1) * n_heads, :] = k_new
        for h in range(n_heads):
            k_all = jnp.concatenate([kc_ref[:, h, :], _rows8(k_new[h:h + 1])], axis=0).astype(BF16)
            sh = lax.dot_general(_rows8(q[h:h + 1]).astype(BF16), k_all, (((1,), (1,)), ((), ())),
                                 preferred_element_type=F32)
            s_ref[h:h + 1, :] = sh[0:1]
        s = jnp.where(lane <= n_keys, s_ref[...] * ATTN_SCALE, -jnp.inf)
        m = jnp.max(s, axis=-1, keepdims=True)
        p = jnp.exp(s - m)
        ms.append(m)
        ls.append(jnp.sum(p, axis=-1, keepdims=True))
        p_ref[...] = p
        for h in range(n_heads):
            v_all = jnp.concatenate([vc_ref[:, h, :], _rows8(v_new[h:h + 1])], axis=0).astype(BF16)
            rh = jnp.dot(_rows8(p_ref[h:h + 1, :]).astype(BF16), v_all, preferred_element_type=F32)
            r_ref[g, h:h + 1, :] = rh[0:1]
    top = jnp.maximum(jnp.maximum(ms[0], ms[1]), ms[2])
    ws = [jnp.exp(m - top) for m in ms]
    num = ws[0] * r_ref[0] + ws[1] * r_ref[1] + ws[2] * r_ref[2]
    den = ws[0] * ls[0] + ws[1] * ls[1] + ws[2] * ls[2]
    o_ref[...] = num / den


def _dilated_step(qkv, caches_k, caches_v, layer, past_len):
    bsz, rows, _ = qkv.shape
    n_heads = caches_k[0].shape[3]
    cos2, sin2 = _rope_tables(jnp.full((1,), past_len, I32))
    args, specs = [], []
    for kc, vc, dil in zip(caches_k, caches_v, A_DILATIONS):
        wb = kc.shape[2]
        assert wb == dil * A_KEYS, "cache must hold exactly the window"
        for cache in (kc, vc):
            args.append(cache.reshape(cache.shape[0], bsz, wb // dil, dil * n_heads, HEAD_DIM))
            specs.append(pl.BlockSpec((None, None, wb // dil, n_heads, HEAD_DIM),
                                      lambda b: (layer, b, 0, 0, 0)))
    slab = lambda r: pl.BlockSpec((None, r, HEAD_DIM), lambda b: (b, 0, 0))
    tab = pl.BlockSpec((1, HEAD_DIM), lambda b: (0, 0))
    return pl.pallas_call(
        _dilated_step_kernel,
        grid=(bsz,),
        in_specs=[slab(rows), tab, tab] + specs,
        out_specs=[slab(n_heads), slab(3 * n_heads)],
        out_shape=[jax.ShapeDtypeStruct((bsz, n_heads, HEAD_DIM), F32),
                   jax.ShapeDtypeStruct((bsz, 3 * n_heads, HEAD_DIM), F32)],
        scratch_shapes=[pltpu.VMEM((n_heads, A_KEYS + 8), F32), pltpu.VMEM((n_heads, A_KEYS + 8), F32),
                        pltpu.VMEM((3, n_heads, HEAD_DIM), F32)],
        compiler_params=_params(1),
        name="dilated_step",
    )(qkv, cos2, sin2, *args)


def _mem_attend_kernel(q_ref, k_ref, v_ref, o_ref):
    n_heads = q_ref.shape[-1] // HEAD_DIM
    for h in range(n_heads):
        sl = slice(h * HEAD_DIM, (h + 1) * HEAD_DIM)
        s = lax.dot_general(q_ref[:, sl].astype(BF16), k_ref[:, sl].astype(BF16), (((1,), (1,)), ((), ())),
                            preferred_element_type=F32) * ATTN_SCALE
        m = jnp.max(s, axis=-1, keepdims=True)
        e = jnp.exp(s - m)
        p = e / jnp.sum(e, axis=-1, keepdims=True)
        o = jnp.dot(p.astype(BF16), v_ref[:, sl].astype(BF16), preferred_element_type=F32)
        o_ref[:, sl] = o.astype(o_ref.dtype)


def _mem_step_kernel(q_ref, k_ref, v_ref, o_ref):
    for h in range(q_ref.shape[0]):
        s = lax.dot_general(_rows8(q_ref[h:h + 1, :]).astype(BF16), k_ref[:, h, :].astype(BF16),
                            (((1,), (1,)), ((), ())), preferred_element_type=F32) * ATTN_SCALE
        m = jnp.max(s, axis=-1, keepdims=True)
        e = jnp.exp(s - m)
        p = e / jnp.sum(e, axis=-1, keepdims=True)
        o = jnp.dot(p.astype(BF16), v_ref[:, h, :].astype(BF16), preferred_element_type=F32)
        o_ref[h:h + 1, :] = o[0:1]


def _mem_attend_prompt(q, kv, bsz, *, tq):
    rows, w = q.shape
    s = rows // bsz
    mem = kv.shape[0] // bsz
    assert s % tq == 0
    nq = s // tq
    return pl.pallas_call(
        _mem_attend_kernel,
        grid=(bsz, nq),
        in_specs=[pl.BlockSpec((tq, w), lambda b, i: (b * nq + i, 0)),
                  pl.BlockSpec((mem, w), lambda b, i: (b, 0)),
                  pl.BlockSpec((mem, w), lambda b, i: (b, 1))],
        out_specs=pl.BlockSpec((tq, w), lambda b, i: (b * nq + i, 0)),
        out_shape=jax.ShapeDtypeStruct((rows, w), BF16),
        compiler_params=_params(2),
        name="mem_attend_prompt",
    )(q, kv, kv)


def _mem_attend_step(q, k_cache, v_cache, layer):
    bsz, n_heads, _ = q.shape
    mem = k_cache.shape[2]
    vec = pl.BlockSpec((None, n_heads, HEAD_DIM), lambda b: (b, 0, 0))
    cache = pl.BlockSpec((None, None, mem, n_heads, HEAD_DIM), lambda b: (layer, b, 0, 0, 0))
    return pl.pallas_call(
        _mem_step_kernel,
        grid=(bsz,),
        in_specs=[vec, cache, cache],
        out_specs=vec,
        out_shape=jax.ShapeDtypeStruct((bsz, n_heads, HEAD_DIM), F32),
        compiler_params=_params(1),
        name="mem_attend_step",
    )(q, k_cache, v_cache)


PROMPT_TM = 1024
STEP_ROWS = 8


def _proj(x, w, layer, *, tn=512, n_cols=None, col0=0):
    tm = PROMPT_TM if x.shape[0] % PROMPT_TM == 0 else x.shape[0]
    return _matmul(x, w, layer, tm=tm, tn=tn, n_cols=n_cols, col0=col0)


def kernel(x_prompt, x_sample, cache_a1_k, cache_a1_v, cache_a2_k, cache_a2_v, cache_a3_k, cache_a3_v,
           cache_b_k, cache_b_v, cache_mem_k, cache_mem_v, page_table, mem_prompt,
           a_w_in, a_w_out, b_w_in, b_w_out, b_logit_bias, mem_w_q, mem_w_kv, mem_w_out, ln_g, ln_b,
           router_w, router_b, moe_w_gu, moe_b_gu, moe_w_down, moe_b_down):
    bp, sp, d = x_prompt.shape
    bs, ts, _ = x_sample.shape
    assert ts == 1 and bs == STEP_ROWS
    depth = ln_g.shape[0]
    alpha = (2 * depth) ** 0.25
    past_len = page_table.shape[1] * PAGE_SIZE
    a_heads = cache_a1_k.shape[3]
    a_gw = a_heads * HEAD_DIM
    b_heads = cache_b_k.shape[3]
    b_gw = b_heads * HEAD_DIM
    mem_len, mem_heads = cache_mem_k.shape[2], cache_mem_k.shape[3]
    mem_w = mem_heads * HEAD_DIM
    caches_k = (cache_a1_k, cache_a2_k, cache_a3_k)
    caches_v = (cache_a1_v, cache_a2_v, cache_a3_v)
    mem_rows = mem_prompt.reshape(bp * mem_len, d).astype(BF16)

    xp = x_prompt.reshape(bp * sp, d)
    xs = x_sample.reshape(bs, d)
    xpb, xsb = xp.astype(BF16), xs
    sorted_rows = jnp.zeros((_moe_sorted_rows(bp * sp + bs, router_w.shape[2]), d // 2), U32)
    pa_k, pa_v, sa_k, sa_v = ([[] for _ in range(3)] for _ in range(4))
    pb_k, pb_v, sb_k, sb_v, pm_k, pm_v = [], [], [], [], [], []

    for i in range(depth):
        j = i // 2
        if i % 2 == 0:
            qkv_p = _proj(xpb, a_w_in, j).reshape(bp, sp, -1)
            qkv_s = _proj(xsb, a_w_in, j).reshape(bs, -1, HEAD_DIM)
            ns, ms, ls = [], [], []
            for g, (win, dil) in enumerate(zip(A_WINDOWS, A_DILATIONS)):
                n_g, m_g, l_g, k_rot = _dilated_prompt(qkv_p, g, dil, a_heads)
                ns.append(n_g.reshape(bp * sp, a_gw))
                ms.append(m_g.reshape(bp * sp, a_heads))
                ls.append(l_g.reshape(bp * sp, a_heads))
                keep = min(win, sp)
                v_g = qkv_p[:, :, (3 * g + 2) * a_gw:(3 * g + 3) * a_gw]
                pa_k[g].append(k_rot[:, sp - keep:].reshape(bp, keep, a_heads, HEAD_DIM))
                pa_v[g].append(v_g[:, sp - keep:].reshape(bp, keep, a_heads, HEAD_DIM))
            op = _merge_groups(ns, ms, ls, tm=256)
            os_, k_new = _dilated_step(qkv_s, caches_k, caches_v, j, past_len)
            for g in range(3):
                sa_k[g].append(k_new[:, g * a_heads:(g + 1) * a_heads].reshape(bs, 1, a_heads, HEAD_DIM))
                sa_v[g].append(qkv_s[:, (3 * g + 2) * a_heads:(3 * g + 3) * a_heads]
                               .reshape(bs, 1, a_heads, HEAD_DIM))
            fp = _proj(op, a_w_out, j)
            fs = _proj(os_.reshape(bs, a_gw), a_w_out, j)
        else:
            qkv_p = _proj(xpb, b_w_in, j).reshape(bp, sp, -1)
            qkv_s = _proj(xsb, b_w_in, j).reshape(bs, 3, b_heads, HEAD_DIM)
            op = _sb_prompt(qkv_p, b_logit_bias[j], b_heads).reshape(bp * sp, b_gw)
            os_ = _sb_step(qkv_s[:, 0], b_logit_bias[j], cache_b_k, cache_b_v, j, page_table)
            pb_k.append(qkv_p[:, :, b_gw:2 * b_gw].reshape(bp, sp, b_heads, HEAD_DIM))
            pb_v.append(qkv_p[:, :, 2 * b_gw:].reshape(bp, sp, b_heads, HEAD_DIM))
            sb_k.append(qkv_s[:, 1].reshape(bs, 1, b_heads, HEAD_DIM))
            sb_v.append(qkv_s[:, 2].reshape(bs, 1, b_heads, HEAD_DIM))
            fp = _proj(op, b_w_out, j)
            fs = _proj(os_.reshape(bs, b_gw), b_w_out, j)
        xp, xpb = _deepnorm(xp, fp, ln_g[i, 0], ln_b[i, 0], alpha, tm=256)
        xs, xsb = _deepnorm(xs, fs, ln_g[i, 0], ln_b[i, 0], alpha, tm=bs)

        kv = _proj(mem_rows, mem_w_kv, i)
        kv4 = kv.reshape(bp, mem_len, 2, mem_heads, HEAD_DIM)
        pm_k.append(kv4[:, :, 0])
        pm_v.append(kv4[:, :, 1])
        qp = _proj(xpb, mem_w_q, i)
        qs = _proj(xsb, mem_w_q, i)
        op = _mem_attend_prompt(qp, kv, bp, tq=512)
        os_ = _mem_attend_step(qs.reshape(bs, mem_heads, HEAD_DIM), cache_mem_k, cache_mem_v, i)
        fp = _proj(op, mem_w_out, i)
        fs = _proj(os_.reshape(bs, mem_w), mem_w_out, i)
        xp, xpb, xp_packed = _deepnorm(xp, fp, ln_g[i, 1], ln_b[i, 1], alpha, tm=256, packed=True)
        xs, xsb, xs_packed = _deepnorm(xs, fs, ln_g[i, 1], ln_b[i, 1], alpha, tm=bs, packed=True)

        xp, xpb, xs, xsb, sorted_rows = _moe_layer(
            xp, xpb, xp_packed, xs, xs_packed, sorted_rows, i, router_w, router_b, moe_w_gu, moe_b_gu,
            moe_w_down, moe_b_down, ln_g[i, 2], ln_b[i, 2], alpha)

    stack = jnp.stack
    return (xp.reshape(bp, sp, d), xs.reshape(bs, ts, d),
            stack(pa_k[0]), stack(pa_v[0]), stack(pa_k[1]), stack(pa_v[1]), stack(pa_k[2]), stack(pa_v[2]),
            stack(pb_k), stack(pb_v), stack(pm_k), stack(pm_v),
            stack(sa_k[0]), stack(sa_v[0]), stack(sa_k[1]), stack(sa_v[1]), stack(sa_k[2]), stack(sa_v[2]),
            stack(sb_k), stack(sb_v))
```

```python
import functools

import jax
import jax.numpy as jnp
import numpy as np
from jax import lax
from jax.experimental import pallas as pl
from jax.experimental.pallas import tpu as pltpu

F32 = jnp.float32
BF16 = jnp.bfloat16
I32 = jnp.int32

HEAD_DIM = 128
ATTN_SCALE = HEAD_DIM ** -0.5
TOP_K = 4
SWIGLU_LIMIT = 7.0
SWIGLU_ALPHA = 1.702
ROPE_THETA = 10000.0
LN_EPS = 1e-5
A_WINDOWS = (128, 512, 2048)
A_DILATIONS = (1, 4, 16)
A_KEYS = 128
BLOCK = 128
PAGE_SIZE = 128

V7X_VMEM_BYTES = 64 * 1024 * 1024
VMEM_LIMIT = (V7X_VMEM_BYTES * 3) // 4
MOSAIC_TEMP_BYTES = 8 * 1024 * 1024
MOE_TM = 256


def _params(n_axes, vmem_limit=VMEM_LIMIT):
    return pltpu.CompilerParams(dimension_semantics=("arbitrary",) * n_axes,
                                vmem_limit_bytes=vmem_limit)


def _block_bytes(*shape_dtypes):
    total = 0
    for shape, dtype in shape_dtypes:
        n = 1
        for s in shape:
            n *= s
        total += 2 * n * jnp.dtype(dtype).itemsize
    return total


def _matmul_kernel(x_ref, w_ref, o_ref, wb_ref):
    @pl.when(pl.program_id(1) == 0)
    def _cast():
        wb_ref[...] = w_ref[...].astype(BF16)

    o_ref[...] = jnp.dot(x_ref[...].astype(BF16), wb_ref[...], preferred_element_type=F32)


def _matmul(x, w, layer, *, tm, tn, n_cols=None, col0=0):
    m, k = x.shape
    n = w.shape[2] if n_cols is None else n_cols
    assert m % tm == 0 and n % tn == 0 and col0 % tn == 0
    cb = col0 // tn
    return pl.pallas_call(
        _matmul_kernel,
        grid=(n // tn, m // tm),
        in_specs=[pl.BlockSpec((tm, k), lambda j, i: (i, 0)),
                  pl.BlockSpec((None, k, tn), lambda j, i: (layer, 0, cb + j))],
        out_specs=pl.BlockSpec((tm, tn), lambda j, i: (i, j)),
        out_shape=jax.ShapeDtypeStruct((m, n), F32),
        scratch_shapes=[pltpu.VMEM((k, tn), BF16)],
        compiler_params=_params(2),
        name="matmul",
    )(x, w)


def _layer_norm_rows(y, g, b):
    mu = jnp.mean(y, axis=-1, keepdims=True)
    d = y - mu
    var = jnp.mean(d * d, axis=-1, keepdims=True)
    return d * lax.rsqrt(var + LN_EPS) * g + b


BF16_ROWS = 16


def _operand_dtype(rows):
    return BF16 if rows % BF16_ROWS == 0 else F32


U32 = jnp.uint32
HIGH_HALF = np.uint32(0xFFFF0000)


def _pack_bf16_pairs(y):
    half = y.shape[1] // 2
    lo = lax.bitcast_convert_type(y[:, :half].astype(BF16).astype(F32), U32) >> 16
    hi = lax.bitcast_convert_type(y[:, half:].astype(BF16).astype(F32), U32) & HIGH_HALF
    return hi | lo


def _unpack_bf16_pairs(w):
    return (lax.bitcast_convert_type(w << 16, F32), lax.bitcast_convert_type(w & HIGH_HALF, F32))


def _deepnorm_kernel(alpha, x_ref, f_ref, g_ref, b_ref, o_ref, ob_ref, *packed_ref):
    y = _layer_norm_rows(alpha * x_ref[...] + f_ref[...], g_ref[...], b_ref[...])
    o_ref[...] = y
    ob_ref[...] = y.astype(ob_ref.dtype)
    if packed_ref:
        packed_ref[0][...] = _pack_bf16_pairs(y)


def _deepnorm(x, f, g, b, alpha, *, tm, packed=False):
    m, d = x.shape
    assert m % tm == 0
    row = pl.BlockSpec((tm, d), lambda i: (i, 0))
    vec = pl.BlockSpec((1, d), lambda i: (0, 0))
    out_specs = [row, row]
    out_shape = [jax.ShapeDtypeStruct((m, d), F32), jax.ShapeDtypeStruct((m, d), _operand_dtype(m))]
    if packed:
        out_specs.append(pl.BlockSpec((tm, d // 2), lambda i: (i, 0)))
        out_shape.append(jax.ShapeDtypeStruct((m, d // 2), U32))
    return pl.pallas_call(
        functools.partial(_deepnorm_kernel, alpha),
        grid=(m // tm,),
        in_specs=[row, row, vec, vec],
        out_specs=out_specs,
        out_shape=out_shape,
        compiler_params=_params(1),
        name="deepnorm",
    )(x, f, g.reshape(1, d), b.reshape(1, d))


def _router_kernel(n_valid, x_ref, wrt_ref, br_ref, cin_ref,
                   idx_ref, gate_ref, rank_ref, cout_ref, carry_ref):
    i = pl.program_id(0)
    n_exp, tm = wrt_ref.shape[0], x_ref.shape[0]

    @pl.when(i == 0)
    def _init():
        carry_ref[...] = cin_ref[...]

    logits = lax.dot_general(wrt_ref[...], x_ref[...].astype(BF16), (((1,), (1,)), ((), ())),
                             preferred_element_type=F32) + br_ref[...]
    e_iota = lax.broadcasted_iota(I32, (n_exp, tm), 0)
    tok = i * tm + lax.broadcasted_iota(I32, (1, tm), 1)
    live = tok < n_valid
    work = logits
    vals, idxs, hots = [], [], []
    for _ in range(TOP_K):
        m = jnp.max(work, axis=0, keepdims=True)
        sel = jnp.min(jnp.where(work == m, e_iota, n_exp), axis=0, keepdims=True)
        hot = e_iota == sel
        vals.append(m)
        idxs.append(sel)
        hots.append(hot)
        work = jnp.where(hot, -jnp.inf, work)
    exps = [jnp.exp(v - vals[0]) for v in vals]
    denom = exps[0] + exps[1] + exps[2] + exps[3]
    any_hot = jnp.where((hots[0] | hots[1] | hots[2] | hots[3]) & live, 1.0, 0.0)
    earlier = (lax.broadcasted_iota(I32, (tm, tm), 0) < lax.broadcasted_iota(I32, (tm, tm), 1))
    before = jnp.dot(any_hot.astype(BF16), jnp.where(earlier, 1.0, 0.0).astype(BF16),
                     preferred_element_type=F32) + carry_ref[...]
    for k in range(TOP_K):
        idx_ref[k:k + 1, :] = idxs[k]
        gate_ref[k:k + 1, :] = exps[k] / denom
        rank_ref[k:k + 1, :] = jnp.sum(jnp.where(hots[k], before, 0.0), axis=0,
                                       keepdims=True).astype(I32)
    carry_ref[...] = carry_ref[...] + jnp.sum(any_hot, axis=1, keepdims=True)
    cout_ref[...] = carry_ref[...]


def _router(x, wrt, br, counts_in, n_valid, *, tm):
    m, d = x.shape
    n_exp = wrt.shape[0]
    assert m % tm == 0
    full = lambda shape: pl.BlockSpec(shape, lambda i: (0, 0))
    out_row = pl.BlockSpec((TOP_K, tm), lambda i: (0, i))
    return pl.pallas_call(
        functools.partial(_router_kernel, n_valid),
        grid=(m // tm,),
        in_specs=[pl.BlockSpec((tm, d), lambda i: (i, 0)), full((n_exp, d)), full((n_exp, 1)),
                  full((n_exp, 1))],
        out_specs=[out_row, out_row, out_row, full((n_exp, 1))],
        out_shape=[jax.ShapeDtypeStruct((TOP_K, m), I32), jax.ShapeDtypeStruct((TOP_K, m), F32),
                   jax.ShapeDtypeStruct((TOP_K, m), I32), jax.ShapeDtypeStruct((n_exp, 1), F32)],
        scratch_shapes=[pltpu.VMEM((n_exp, 1), F32)],
        compiler_params=_params(1),
        name="router",
    )(x, wrt, br, counts_in)


def _dispatch_kernel(pos_ref, x_ref, xs_in_ref, xs_ref, sem):
    del xs_in_ref
    tm = x_ref.shape[0]

    def row_copy(r, k):
        p = pos_ref[0, 0, k * tm + r]
        return pltpu.make_async_copy(x_ref.at[pl.ds(r, 1)], xs_ref.at[pl.ds(p, 1)], sem)

    def issue(r, carry):
        for k in range(TOP_K):
            row_copy(r, k).start()
        return carry

    lax.fori_loop(0, tm, issue, 0)

    def drain(r, carry):
        for k in range(TOP_K):
            row_copy(r, k).wait()
        return carry

    lax.fori_loop(0, tm, drain, 0)


def _dispatch(x, pos, xs, *, tm):
    m, d = x.shape
    assert m % tm == 0
    nt = m // tm
    pos_t = pos.reshape(TOP_K, nt, tm).transpose(1, 0, 2).reshape(nt, 1, TOP_K * tm)
    return pl.pallas_call(
        _dispatch_kernel,
        grid=(nt,),
        in_specs=[pl.BlockSpec((1, 1, TOP_K * tm), lambda i: (i, 0, 0), memory_space=pltpu.SMEM),
                  pl.BlockSpec((tm, d), lambda i: (i, 0)),
                  pl.BlockSpec(memory_space=pl.ANY)],
        out_specs=pl.BlockSpec(memory_space=pl.ANY),
        out_shape=jax.ShapeDtypeStruct(xs.shape, xs.dtype),
        scratch_shapes=[pltpu.SemaphoreType.DMA(())],
        input_output_aliases={2: 0},
        compiler_params=_params(1),
        name="moe_dispatch",
    )(pos_t, x, xs)


def _tile_is_first(te_ref, i):
    return jnp.logical_or(i == 0, te_ref[i] != te_ref[jnp.maximum(i - 1, 0)])


def _gmm1_kernel(te_ref, tr_ref, x_ref, wg_ref, wl_ref, bg_ref, bl_ref, h_ref, wgb_ref, wlb_ref):
    i = pl.program_id(1)

    @pl.when(_tile_is_first(te_ref, i))
    def _cast():
        wgb_ref[...] = wg_ref[...].astype(BF16)
        wlb_ref[...] = wl_ref[...].astype(BF16)

    rows = tr_ref[i]

    @pl.when(rows > 0)
    def _compute():
        tm, half = x_ref.shape
        keep = lax.broadcasted_iota(I32, (tm, 1), 0) < rows
        x_lo, x_hi = _unpack_bf16_pairs(x_ref[...])
        x_lo = jnp.where(keep, x_lo, 0.0).astype(BF16)
        x_hi = jnp.where(keep, x_hi, 0.0).astype(BF16)

        def project(w_ref, b_ref):
            return (jnp.dot(x_lo, w_ref[0:half, :], preferred_element_type=F32)
                    + jnp.dot(x_hi, w_ref[half:2 * half, :], preferred_element_type=F32) + b_ref[...])

        gate = project(wgb_ref, bg_ref)
        lin = project(wlb_ref, bl_ref)
        glu = jnp.minimum(gate, SWIGLU_LIMIT)
        lin = jnp.clip(lin, -SWIGLU_LIMIT, SWIGLU_LIMIT)
        h = glu * jax.nn.sigmoid(SWIGLU_ALPHA * glu) * (lin + 1.0)
        h_ref[...] = h.astype(BF16)

    @pl.when(rows <= 0)
    def _blank():
        h_ref[...] = jnp.zeros(h_ref.shape, BF16)


def _gmm1(xs, w_gu, b_gu, layer, tile_expert, tile_rows, *, tn):
    rows, half = xs.shape
    d = 2 * half
    ff = w_gu.shape[3] // 2
    n_exp = w_gu.shape[1]
    nt = rows // MOE_TM
    assert ff % tn == 0
    nc = ff // tn
    b4 = b_gu.reshape(b_gu.shape[0], n_exp, 1, 2 * ff)
    grid_spec = pltpu.PrefetchScalarGridSpec(
        num_scalar_prefetch=2,
        grid=(nc, nt),
        in_specs=[
            pl.BlockSpec((MOE_TM, half), lambda j, i, te, tr: (i, 0)),
            pl.BlockSpec((None, None, d, tn), lambda j, i, te, tr: (layer, te[i], 0, j)),
            pl.BlockSpec((None, None, d, tn), lambda j, i, te, tr: (layer, te[i], 0, nc + j)),
            pl.BlockSpec((None, None, 1, tn), lambda j, i, te, tr: (layer, te[i], 0, j)),
            pl.BlockSpec((None, None, 1, tn), lambda j, i, te, tr: (layer, te[i], 0, nc + j)),
        ],
        out_specs=pl.BlockSpec((MOE_TM, tn), lambda j, i, te, tr: (i, j)),
        scratch_shapes=[pltpu.VMEM((d, tn), BF16), pltpu.VMEM((d, tn), BF16)],
    )
    return pl.pallas_call(
        _gmm1_kernel,
        grid_spec=grid_spec,
        out_shape=jax.ShapeDtypeStruct((rows, ff), BF16),
        compiler_params=_params(2, _block_bytes(((MOE_TM, half), U32), ((d, tn), F32), ((d, tn), F32),
                                                ((d, tn), BF16), ((MOE_TM, tn), BF16))
                                + MOSAIC_TEMP_BYTES),
        name="moe_gmm1",
    )(tile_expert, tile_rows, xs, w_gu, w_gu, b4, b4)


def _gmm2_kernel(te_ref, tr_ref, h_ref, w_ref, b_ref, y_ref, wb_ref):
    i = pl.program_id(1)

    @pl.when(_tile_is_first(te_ref, i))
    def _cast():
        wb_ref[...] = w_ref[...].astype(BF16)

    @pl.when(tr_ref[i] > 0)
    def _compute():
        y_ref[...] = jnp.dot(h_ref[...], wb_ref[...], preferred_element_type=F32) + b_ref[...]

    @pl.when(tr_ref[i] <= 0)
    def _blank():
        y_ref[...] = jnp.zeros(y_ref.shape, F32)


def _gmm2(h, w_d, b_d, layer, tile_expert, tile_rows, *, tn):
    rows, ff = h.shape
    n_exp, d = w_d.shape[1], w_d.shape[3]
    nt = rows // MOE_TM
    assert d % tn == 0
    b4 = b_d.reshape(b_d.shape[0], n_exp, 1, d)
    grid_spec = pltpu.PrefetchScalarGridSpec(
        num_scalar_prefetch=2,
        grid=(d // tn, nt),
        in_specs=[
            pl.BlockSpec((MOE_TM, ff), lambda j, i, te, tr: (i, 0)),
            pl.BlockSpec((None, None, ff, tn), lambda j, i, te, tr: (layer, te[i], 0, j)),
            pl.BlockSpec((None, None, 1, tn), lambda j, i, te, tr: (layer, te[i], 0, j)),
        ],
        out_specs=pl.BlockSpec((MOE_TM, tn), lambda j, i, te, tr: (i, j)),
        scratch_shapes=[pltpu.VMEM((ff, tn), BF16)],
    )
    return pl.pallas_call(
        _gmm2_kernel,
        grid_spec=grid_spec,
        out_shape=jax.ShapeDtypeStruct((rows, d), F32),
        compiler_params=_params(2),
        name="moe_gmm2",
    )(tile_expert, tile_rows, h, w_d, b4)


def _combine_kernel(alpha, pos_ref, next_pos_ref, x_ref, gate_ref, g_ref, b_ref, y_ref, o_ref, ob_ref,
                    buf_ref, sem):
    i = pl.program_id(0)
    tm = x_ref.shape[0]
    slot = i % 2

    def row_copy(p_ref, s, r, k):
        p = p_ref[0, 0, k * tm + r]
        return pltpu.make_async_copy(y_ref.at[pl.ds(p, 1)], buf_ref.at[s, k, pl.ds(r, 1)], sem.at[s])

    def start_tile(p_ref, s):
        def body(r, carry):
            for k in range(TOP_K):
                row_copy(p_ref, s, r, k).start()
            return carry
        lax.fori_loop(0, tm, body, 0)

    @pl.when(i == 0)
    def _first():
        start_tile(pos_ref, 0)

    @pl.when(i + 1 < pl.num_programs(0))
    def _prefetch():
        start_tile(next_pos_ref, 1 - slot)

    def wait_row(r, carry):
        for k in range(TOP_K):
            row_copy(pos_ref, slot, r, k).wait()
        return carry

    lax.fori_loop(0, tm, wait_row, 0)

    gates = gate_ref[...]
    f = gates[:, 0:1] * buf_ref[slot, 0]
    for k in range(1, TOP_K):
        f = f + gates[:, k:k + 1] * buf_ref[slot, k]
    y = _layer_norm_rows(alpha * x_ref[...] + f, g_ref[...], b_ref[...])
    o_ref[...] = y
    ob_ref[...] = y.astype(ob_ref.dtype)


def _combine(x, gates, pos, ys, g, b, alpha, *, tm):
    m, d = x.shape
    assert m % tm == 0
    nt = m // tm
    pos_t = pos.reshape(TOP_K, nt, tm).transpose(1, 0, 2).reshape(nt, 1, TOP_K * tm)
    row = pl.BlockSpec((tm, d), lambda i: (i, 0))
    vec = pl.BlockSpec((1, d), lambda i: (0, 0))
    return pl.pallas_call(
        functools.partial(_combine_kernel, alpha),
        grid=(nt,),
        in_specs=[pl.BlockSpec((1, 1, TOP_K * tm), lambda i: (i, 0, 0), memory_space=pltpu.SMEM),
                  pl.BlockSpec((1, 1, TOP_K * tm), lambda i: (jnp.minimum(i + 1, nt - 1), 0, 0),
                               memory_space=pltpu.SMEM),
                  row, pl.BlockSpec((tm, TOP_K), lambda i: (i, 0)), vec, vec,
                  pl.BlockSpec(memory_space=pl.ANY)],
        out_specs=[row, row],
        out_shape=[jax.ShapeDtypeStruct((m, d), F32), jax.ShapeDtypeStruct((m, d), _operand_dtype(m))],
        scratch_shapes=[pltpu.VMEM((2, TOP_K, tm, d), F32), pltpu.SemaphoreType.DMA((2,))],
        compiler_params=_params(1),
        name="moe_combine",
    )(pos_t, pos_t, x, gates.T, g.reshape(1, d), b.reshape(1, d), ys)


def _moe_sorted_rows(n_tokens, n_exp):
    return ((n_tokens * TOP_K) // MOE_TM + n_exp) * MOE_TM


def _moe_layer(xp, xpb, xp_packed, xs, xs_packed, sorted_rows, layer, router_w, router_b,
               w_gu, b_gu, w_d, b_d, g, b, alpha):
    mp, d = xp.shape
    ms = xs.shape[0]
    n_exp = router_w.shape[2]
    wrt = router_w[layer].T.astype(BF16)
    br = router_b[layer].reshape(n_exp, 1)
    pad = 128
    xs_pad = jnp.zeros((pad, d), F32).at[:ms].set(xs)
    idx_p, gate_p, rank_p, cnt_p = _router(xpb, wrt, br, jnp.zeros((n_exp, 1), F32), mp, tm=512)
    idx_s, gate_s, rank_s, cnt = _router(xs_pad, wrt, br, cnt_p, ms, tm=pad)
    idx = jnp.concatenate([idx_p, idx_s[:, :ms]], axis=1)
    rank = jnp.concatenate([rank_p, rank_s[:, :ms]], axis=1)

    counts = cnt[:, 0].astype(I32)
    tiles = (counts + MOE_TM - 1) // MOE_TM
    tile_end = jnp.cumsum(tiles)
    tile_start = tile_end - tiles
    experts = jnp.arange(n_exp, dtype=I32)
    row_start = jnp.sum(jnp.where(idx[:, :, None] == experts, tile_start * MOE_TM, 0), axis=-1)
    pos = row_start + rank
    nt = sorted_rows.shape[0] // MOE_TM
    t_iota = jnp.arange(nt, dtype=I32)
    n_used = tile_end[-1]
    t_eff = jnp.minimum(t_iota, jnp.maximum(n_used - 1, 0))
    tile_expert = jnp.minimum(jnp.sum((tile_end[None, :] <= t_eff[:, None]).astype(I32), axis=1), n_exp - 1)
    hot = tile_expert[:, None] == experts
    tile_rows = (jnp.sum(jnp.where(hot, counts, 0), axis=1)
                 - (t_iota - jnp.sum(jnp.where(hot, tile_start, 0), axis=1)) * MOE_TM)
    tile_rows = jnp.where(t_iota < n_used, jnp.clip(tile_rows, 0, MOE_TM), 0).astype(I32)

    sorted_rows = _dispatch(xp_packed, pos[:, :mp], sorted_rows, tm=256)
    sorted_rows = _dispatch(xs_packed, pos[:, mp:], sorted_rows, tm=ms)
    h = _gmm1(sorted_rows, w_gu, b_gu, layer, tile_expert, tile_rows, tn=min(512, w_d.shape[2]))
    ys = _gmm2(h, w_d, b_d, layer, tile_expert, tile_rows, tn=min(2048, d))
    yp, ypb = _combine(xp, gate_p, pos[:, :mp], ys, g, b, alpha, tm=64)
    ysn, ysb = _combine(xs, gate_s[:, :ms], pos[:, mp:], ys, g, b, alpha, tm=ms)
    return yp, ypb, ysn, ysb, sorted_rows


def _rope_tables(pos):
    half = HEAD_DIM // 2
    inv_freq = ROPE_THETA ** (-jnp.arange(half, dtype=F32) / half)
    ang = pos.astype(F32)[..., None] * inv_freq
    cos, sin = jnp.cos(ang), jnp.sin(ang)
    return jnp.concatenate([cos, cos], axis=-1), jnp.concatenate([-sin, sin], axis=-1)


def _rope(x, cos2, sin2):
    width = x.shape[-1]
    half = HEAD_DIM // 2
    axis = x.ndim - 1
    if width == HEAD_DIM:
        return x * cos2 + pltpu.roll(x, half, axis=axis) * sin2
    first = lax.broadcasted_iota(I32, x.shape, axis) % HEAD_DIM < half
    partner = jnp.where(first, pltpu.roll(x, width - half, axis=axis), pltpu.roll(x, half, axis=axis))
    return x * cos2 + partner * sin2


A_HB = 2


def _dilated_kernel(q1_ref, k1_ref, v1_ref, q2_ref, k2_ref, v2_ref, q3_ref, k3_ref, v3_ref,
                    cos_ref, sin_ref, o_ref, kr1_ref, kr2_ref, kr3_ref, kb_ref, vb_ref):
    i = pl.program_id(2)
    s_len = k1_ref.shape[0]
    n_blocks = s_len // BLOCK
    groups = ((q1_ref, k1_ref, v1_ref, kr1_ref), (q2_ref, k2_ref, v2_ref, kr2_ref),
              (q3_ref, k3_ref, v3_ref, kr3_ref))

    @pl.when(i == 0)
    def _prepare():
        for g, (_, k_ref, v_ref, _) in enumerate(groups):
            kb_ref[g] = _rope(k_ref[...], cos_ref[...], sin_ref[...]).astype(BF16)
            vb_ref[g] = v_ref[...].astype(BF16)

    row0 = pl.multiple_of(i * BLOCK, BLOCK)
    cos_q, sin_q = cos_ref[pl.ds(row0, BLOCK), :], sin_ref[pl.ds(row0, BLOCK), :]
    stats = [[None] * 3 for _ in range(A_HB)]
    for g, ((q_ref, k_ref, _, kr_ref), dil) in enumerate(zip(groups, A_DILATIONS)):
        kr_ref[...] = _rope(k_ref[pl.ds(row0, BLOCK), :], cos_q, sin_q)
        q_rot = _rope(q_ref[...], cos_q, sin_q).astype(BF16)
        n_keys = min((dil + 1) * BLOCK, s_len)
        kb0 = jnp.clip(i + 1 - n_keys // BLOCK, 0, n_blocks - n_keys // BLOCK)
        key0 = pl.multiple_of(kb0 * BLOCK, BLOCK)
        dist = ((row0 + lax.broadcasted_iota(I32, (BLOCK, n_keys), 0))
                - (key0 + lax.broadcasted_iota(I32, (BLOCK, n_keys), 1)))
        assert dil & (dil - 1) == 0
        valid = (dist >= 0) & (dist <= A_KEYS * dil) & ((dist & (dil - 1)) == 0)
        for h in range(A_HB):
            sl = slice(h * HEAD_DIM, (h + 1) * HEAD_DIM)
            k = kb_ref[g, pl.ds(key0, n_keys), sl]
            v = vb_ref[g, pl.ds(key0, n_keys), sl]
            s = lax.dot_general(q_rot[:, sl], k, (((1,), (1,)), ((), ())),
                                preferred_element_type=F32) * ATTN_SCALE
            s = jnp.where(valid, s, -jnp.inf)
            m = jnp.max(s, axis=-1, keepdims=True)
            p = jnp.exp(s - m)
            l = jnp.sum(p, axis=-1, keepdims=True)
            stats[h][g] = (m, l, jnp.dot(p.astype(BF16), v, preferred_element_type=F32))
    for h in range(A_HB):
        (m1, l1, n1), (m2, l2, n2), (m3, l3, n3) = stats[h]
        top = jnp.maximum(jnp.maximum(m1, m2), m3)
        w1, w2, w3 = jnp.exp(m1 - top), jnp.exp(m2 - top), jnp.exp(m3 - top)
        num = w1 * n1 + w2 * n2 + w3 * n3
        den = w1 * l1 + w2 * l2 + w3 * l3
        o_ref[:, h * HEAD_DIM:(h + 1) * HEAD_DIM] = (num / den).astype(BF16)


def _dilated_prompt(qkv, n_heads):
    bsz, s, c = qkv.shape
    assert s % BLOCK == 0 and n_heads % A_HB == 0
    hw = A_HB * HEAD_DIM
    gw = n_heads * HEAD_DIM
    nhb = n_heads // A_HB
    cos2, sin2 = _rope_tables(jnp.arange(s, dtype=I32))
    cos2, sin2 = jnp.tile(cos2, (1, A_HB)), jnp.tile(sin2, (1, A_HB))
    in_specs, args = [], []
    for g in range(3):
        c0 = [(3 * g + part) * gw // hw for part in range(3)]
        in_specs += [pl.BlockSpec((None, BLOCK, hw), lambda b, h, i, c0=c0[0]: (b, i, c0 + h)),
                     pl.BlockSpec((None, s, hw), lambda b, h, i, c0=c0[1]: (b, 0, c0 + h)),
                     pl.BlockSpec((None, s, hw), lambda b, h, i, c0=c0[2]: (b, 0, c0 + h))]
        args += [qkv, qkv, qkv]
    tab = pl.BlockSpec((s, hw), lambda b, h, i: (0, 0))
    blk = pl.BlockSpec((None, BLOCK, hw), lambda b, h, i: (b, i, h))
    rows = jax.ShapeDtypeStruct((bsz, s, gw), F32)
    return pl.pallas_call(
        _dilated_kernel,
        grid=(bsz, nhb, s // BLOCK),
        in_specs=in_specs + [tab, tab],
        out_specs=[blk, blk, blk, blk],
        out_shape=[jax.ShapeDtypeStruct((bsz, s, gw), BF16), rows, rows, rows],
        scratch_shapes=[pltpu.VMEM((3, s, hw), BF16), pltpu.VMEM((3, s, hw), BF16)],
        compiler_params=_params(3),
        name="dilated_prompt",
    )(*args, cos2, sin2)


def _log_keep(z):
    return -(jnp.maximum(z, 0.0) + jnp.log1p(jnp.exp(-jnp.abs(z))))


def _split2(x):
    hi = x.astype(BF16)
    return hi, (x - hi.astype(F32)).astype(BF16)


def _dot_split(x, w):
    hi, lo = _split2(x)
    return jnp.dot(hi, w, preferred_element_type=F32) + jnp.dot(lo, w, preferred_element_type=F32)


SB_HB = 4
SB_TQ = 512


def _sb_prompt_kernel(bias_ref, q_ref, k_ref, v_ref, o_ref, kb_ref, vb_ref, acc_ref, run_ref):
    hb_idx = pl.program_id(1)
    qi = pl.program_id(2)
    n_sub = SB_TQ // BLOCK

    @pl.when(qi == 0)
    def _cast():
        kb_ref[...] = k_ref[...].astype(BF16)
        vb_ref[...] = v_ref[...].astype(BF16)

    r = lax.broadcasted_iota(I32, (BLOCK, 2 * BLOCK), 0)
    c = lax.broadcasted_iota(I32, (BLOCK, 2 * BLOCK), 1)
    tri_ext = jnp.where((r >= c) | (c >= BLOCK), 1.0, 0.0).astype(BF16)
    acc_ref[...] = jnp.zeros(acc_ref.shape, F32)
    run_ref[...] = jnp.zeros(run_ref.shape, F32)

    def key_block(off, r0, mask):
        for h in range(SB_HB):
            sl = slice(h * HEAD_DIM, (h + 1) * HEAD_DIM)
            q = q_ref[r0:, sl].astype(BF16)
            k = kb_ref[pl.ds(off, BLOCK), sl]
            v = vb_ref[pl.ds(off, BLOCK), sl]
            z = lax.dot_general(q, k, (((1,), (1,)), ((), ())),
                                preferred_element_type=F32) * ATTN_SCALE + bias_ref[hb_idx * SB_HB + h]
            lk = _log_keep(z)
            if mask is not None:
                lk = jnp.where(mask, lk, 0.0)
            ext = _dot_split(lk, tri_ext)
            a = jnp.exp(z + ext[:, :BLOCK] + run_ref[h, r0:, :])
            if mask is not None:
                a = jnp.where(mask, a, 0.0)
            acc_ref[h, r0:, :] += jnp.dot(a.astype(BF16), v, preferred_element_type=F32)
            run_ref[h, r0:, :] += ext[:, BLOCK:]

    for cb in reversed(range(n_sub)):
        r0 = cb * BLOCK
        rows = SB_TQ - r0
        mask = (lax.broadcasted_iota(I32, (rows, BLOCK), 1) < lax.broadcasted_iota(I32, (rows, BLOCK), 0))
        key_block(pl.multiple_of((qi * n_sub + cb) * BLOCK, BLOCK), r0, mask)

    def earlier(t, carry):
        key_block(pl.multiple_of((qi * n_sub - 1 - t) * BLOCK, BLOCK), 0, None)
        return carry

    lax.fori_loop(0, qi * n_sub, earlier, 0)
    for h in range(SB_HB):
        o_ref[:, h * HEAD_DIM:(h + 1) * HEAD_DIM] = acc_ref[h].astype(BF16)


def _sb_prompt(qkv, bias, n_heads):
    bsz, s, _ = qkv.shape
    assert s % SB_TQ == 0 and n_heads % SB_HB == 0
    hw = SB_HB * HEAD_DIM
    nhb = n_heads // SB_HB
    return pl.pallas_call(
        _sb_prompt_kernel,
        grid=(bsz, nhb, s // SB_TQ),
        in_specs=[pl.BlockSpec(memory_space=pltpu.SMEM),
                  pl.BlockSpec((None, SB_TQ, hw), lambda b, h, i: (b, i, h)),
                  pl.BlockSpec((None, s, hw), lambda b, h, i: (b, 0, nhb + h)),
                  pl.BlockSpec((None, s, hw), lambda b, h, i: (b, 0, 2 * nhb + h))],
        out_specs=pl.BlockSpec((None, SB_TQ, hw), lambda b, h, i: (b, i, h)),
        out_shape=jax.ShapeDtypeStruct((bsz, s, n_heads * HEAD_DIM), BF16),
        scratch_shapes=[pltpu.VMEM((s, hw), BF16), pltpu.VMEM((s, hw), BF16),
                        pltpu.VMEM((SB_HB, SB_TQ, BLOCK), F32), pltpu.VMEM((SB_HB, SB_TQ, BLOCK), F32)],
        compiler_params=_params(3),
        name="stick_breaking_prompt",
    )(bias, qkv, qkv, qkv)


def _rows8(row):
    return jnp.broadcast_to(row, (8, row.shape[1]))


def _sb_step_kernel(pt_ref, q_ref, bias_ref, k_ref, v_ref, o_ref, acc_ref, run_ref, z_ref):
    del pt_ref
    j = pl.program_id(1)
    n_heads = q_ref.shape[0]
    n_rows = k_ref.shape[0] // 128

    @pl.when(j == 0)
    def _init():
        acc_ref[...] = jnp.zeros(acc_ref.shape, F32)
        run_ref[...] = jnp.zeros(run_ref.shape, F32)

    own = (lax.broadcasted_iota(I32, (n_heads, 128), 1) % n_heads
           == lax.broadcasted_iota(I32, (n_heads, 128), 0))
    s_all = lax.dot_general(q_ref[...].astype(BF16), k_ref[...].astype(BF16), (((1,), (1,)), ((), ())),
                            preferred_element_type=F32)
    for c in range(n_rows):
        z_ref[c:c + 1, :] = jnp.sum(jnp.where(own, s_all[:, c * 128:(c + 1) * 128], 0.0),
                                    axis=0, keepdims=True)
    z = z_ref[...] * ATTN_SCALE + bias_ref[...]
    lk = _log_keep(z)

    li = lax.broadcasted_iota(I32, (128, 256), 0)
    lj = lax.broadcasted_iota(I32, (128, 256), 1)
    same_head = li % n_heads == lj % n_heads
    w_ext = jnp.where(same_head & ((li // n_heads >= lj // n_heads) | (lj >= 128)), 1.0, 0.0).astype(BF16)
    ext = _dot_split(lk, w_ext)
    tot = ext[:, 128:]
    later_rows = jnp.where(lax.broadcasted_iota(I32, (n_rows, n_rows), 1)
                           > lax.broadcasted_iota(I32, (n_rows, n_rows), 0), 1.0, 0.0).astype(BF16)
    tot_hi, tot_lo = _split2(tot)
    after = (ext[:, :128] + jnp.dot(later_rows, tot_hi, preferred_element_type=F32)
             + jnp.dot(later_rows, tot_lo, preferred_element_type=F32))
    a = jnp.exp(z + after + run_ref[...])
    run_ref[...] += jnp.sum(tot, axis=0, keepdims=True)

    a_exp = jnp.concatenate(
        [jnp.where(own, jnp.broadcast_to(a[c:c + 1, :], (n_heads, 128)), 0.0) for c in range(n_rows)],
        axis=1).astype(BF16)
    acc_ref[...] += jnp.dot(a_exp, v_ref[...].astype(BF16), preferred_element_type=F32)

    @pl.when(j == pl.num_programs(1) - 1)
    def _done():
        o_ref[...] = acc_ref[...]


def _sb_step(q, bias, k_pool, v_pool, layer, page_table):
    bsz, n_heads, _ = q.shape
    n_pages = page_table.shape[1]
    n_layers, n_pool, page = k_pool.shape[:3]
    assert 128 % n_heads == 0 and (page * n_heads) % 128 == 0
    rows = page * n_heads
    pool = pl.BlockSpec((None, None, rows, HEAD_DIM),
                        lambda b, j, pt: (layer, pt[b, n_pages - 1 - j], 0, 0))
    vec = pl.BlockSpec((None, n_heads, HEAD_DIM), lambda b, j, pt: (b, 0, 0))
    grid_spec = pltpu.PrefetchScalarGridSpec(
        num_scalar_prefetch=1,
        grid=(bsz, n_pages),
        in_specs=[vec, pl.BlockSpec((1, 128), lambda b, j, pt: (0, 0)), pool, pool],
        out_specs=vec,
        scratch_shapes=[pltpu.VMEM((n_heads, HEAD_DIM), F32), pltpu.VMEM((1, 128), F32),
                        pltpu.VMEM((rows // 128, 128), F32)],
    )
    return pl.pallas_call(
        _sb_step_kernel,
        grid_spec=grid_spec,
        out_shape=jax.ShapeDtypeStruct((bsz, n_heads, HEAD_DIM), F32),
        compiler_params=_params(2),
        name="stick_breaking_step",
    )(page_table, q, jnp.tile(bias, 128 // n_heads).reshape(1, 128),
      k_pool.reshape(n_layers, n_pool, rows, HEAD_DIM), v_pool.reshape(n_layers, n_pool, rows, HEAD_DIM))


def _dilated_step_kernel(qkv_ref, cos_ref, sin_ref, k1_ref, v1_ref, k2_ref, v2_ref, k3_ref, v3_ref,
                         o_ref, kr_ref, s_ref, p_ref, r_ref):
    n_keys, n_heads, _ = k1_ref.shape
    cos2, sin2 = cos_ref[...], sin_ref[...]
    caches = ((k1_ref, v1_ref), (k2_ref, v2_ref), (k3_ref, v3_ref))
    lane = lax.broadcasted_iota(I32, (n_heads, n_keys + 8), 1)
    ms, ls = [], []
    for g, (kc_ref, vc_ref) in enumerate(caches):
        base = g * 3 * n_heads
        q = _rope(qkv_ref[base:base + n_heads, :], cos2, sin2)
        k_new = _rope(qkv_ref[base + n_heads:base + 2 * n_heads, :], cos2, sin2)
        v_new = qkv_ref[base + 2 * n_heads:base + 3 * n_heads, :]
        kr_ref[g * n_heads:(g + 1) * n_heads, :] = k_new
        for h in range(n_heads):
            k_all = jnp.concatenate([kc_ref[:, h, :], _rows8(k_new[h:h + 1])], axis=0).astype(BF16)
            sh = lax.dot_general(_rows8(q[h:h + 1]).astype(BF16), k_all, (((1,), (1,)), ((), ())),
                                 preferred_element_type=F32)
            s_ref[h:h + 1, :] = sh[0:1]
        s = jnp.where(lane <= n_keys, s_ref[...] * ATTN_SCALE, -jnp.inf)
        m = jnp.max(s, axis=-1, keepdims=True)
        p = jnp.exp(s - m)
        ms.append(m)
        ls.append(jnp.sum(p, axis=-1, keepdims=True))
        p_ref[...] = p
        for h in range(n_heads):
            v_all = jnp.concatenate([vc_ref[:, h, :], _rows8(v_new[h:h + 1])], axis=0).astype(BF16)
            rh = jnp.dot(_rows8(p_ref[h:h + 1, :]).astype(BF16), v_all, preferred_element_type=F32)
            r_ref[g, h:h + 1, :] = rh[0:1]
    top = jnp.maximum(jnp.maximum(ms[0], ms[1]), ms[2])
    ws = [jnp.exp(m - top) for m in ms]
    num = ws[0] * r_ref[0] + ws[1] * r_ref[1] + ws[2] * r_ref[2]
    den = ws[0] * ls[0] + ws[1] * ls[1] + ws[2] * ls[2]
    o_ref[...] = num / den


def _dilated_step(qkv, caches_k, caches_v, layer, past_len):
    bsz, rows, _ = qkv.shape
    n_heads = caches_k[0].shape[3]
    cos2, sin2 = _rope_tables(jnp.full((1,), past_len, I32))
    args, specs = [], []
    for kc, vc, dil in zip(caches_k, caches_v, A_DILATIONS):
        wb = kc.shape[2]
        assert wb == dil * A_KEYS, "cache must hold exactly the window"
        for cache in (kc, vc):
            args.append(cache.reshape(cache.shape[0], bsz, wb // dil, dil * n_heads, HEAD_DIM))
            specs.append(pl.BlockSpec((None, None, wb // dil, n_heads, HEAD_DIM),
                                      lambda b: (layer, b, 0, 0, 0)))
    slab = lambda r: pl.BlockSpec((None, r, HEAD_DIM), lambda b: (b, 0, 0))
    tab = pl.BlockSpec((1, HEAD_DIM), lambda b: (0, 0))
    return pl.pallas_call(
        _dilated_step_kernel,
        grid=(bsz,),
        in_specs=[slab(rows), tab, tab] + specs,
        out_specs=[slab(n_heads), slab(3 * n_heads)],
        out_shape=[jax.ShapeDtypeStruct((bsz, n_heads, HEAD_DIM), F32),
                   jax.ShapeDtypeStruct((bsz, 3 * n_heads, HEAD_DIM), F32)],
        scratch_shapes=[pltpu.VMEM((n_heads, A_KEYS + 8), F32), pltpu.VMEM((n_heads, A_KEYS + 8), F32),
                        pltpu.VMEM((3, n_heads, HEAD_DIM), F32)],
        compiler_params=_params(1),
        name="dilated_step",
    )(qkv, cos2, sin2, *args)


def _mem_attend_kernel(q_ref, k_ref, v_ref, o_ref):
    n_heads = q_ref.shape[-1] // HEAD_DIM
    for h in range(n_heads):
        sl = slice(h * HEAD_DIM, (h + 1) * HEAD_DIM)
        s = lax.dot_general(q_ref[:, sl].astype(BF16), k_ref[:, sl].astype(BF16), (((1,), (1,)), ((), ())),
                            preferred_element_type=F32) * ATTN_SCALE
        m = jnp.max(s, axis=-1, keepdims=True)
        e = jnp.exp(s - m)
        p = e / jnp.sum(e, axis=-1, keepdims=True)
        o = jnp.dot(p.astype(BF16), v_ref[:, sl].astype(BF16), preferred_element_type=F32)
        o_ref[:, sl] = o.astype(o_ref.dtype)


def _mem_step_kernel(q_ref, k_ref, v_ref, o_ref):
    for h in range(q_ref.shape[0]):
        s = lax.dot_general(_rows8(q_ref[h:h + 1, :]).astype(BF16), k_ref[:, h, :].astype(BF16),
                            (((1,), (1,)), ((), ())), preferred_element_type=F32) * ATTN_SCALE
        m = jnp.max(s, axis=-1, keepdims=True)
        e = jnp.exp(s - m)
        p = e / jnp.sum(e, axis=-1, keepdims=True)
        o = jnp.dot(p.astype(BF16), v_ref[:, h, :].astype(BF16), preferred_element_type=F32)
        o_ref[h:h + 1, :] = o[0:1]


def _mem_attend_prompt(q, kv, bsz, *, tq):
    rows, w = q.shape
    s = rows // bsz
    mem = kv.shape[0] // bsz
    assert s % tq == 0
    nq = s // tq
    return pl.pallas_call(
        _mem_attend_kernel,
        grid=(bsz, nq),
        in_specs=[pl.BlockSpec((tq, w), lambda b, i: (b * nq + i, 0)),
                  pl.BlockSpec((mem, w), lambda b, i: (b, 0)),
                  pl.BlockSpec((mem, w), lambda b, i: (b, 1))],
        out_specs=pl.BlockSpec((tq, w), lambda b, i: (b * nq + i, 0)),
        out_shape=jax.ShapeDtypeStruct((rows, w), BF16),
        compiler_params=_params(2),
        name="mem_attend_prompt",
    )(q, kv, kv)


def _mem_attend_step(q, k_cache, v_cache, layer):
    bsz, n_heads, _ = q.shape
    mem = k_cache.shape[2]
    vec = pl.BlockSpec((None, n_heads, HEAD_DIM), lambda b: (b, 0, 0))
    cache = pl.BlockSpec((None, None, mem, n_heads, HEAD_DIM), lambda b: (layer, b, 0, 0, 0))
    return pl.pallas_call(
        _mem_step_kernel,
        grid=(bsz,),
        in_specs=[vec, cache, cache],
        out_specs=vec,
        out_shape=jax.ShapeDtypeStruct((bsz, n_heads, HEAD_DIM), F32),
        compiler_params=_params(1),
        name="mem_attend_step",
    )(q, k_cache, v_cache)


PROMPT_TM = 1024
STEP_ROWS = 8


def _proj(x, w, layer, *, tn=512, n_cols=None, col0=0):
    tm = PROMPT_TM if x.shape[0] % PROMPT_TM == 0 else x.shape[0]
    return _matmul(x, w, layer, tm=tm, tn=tn, n_cols=n_cols, col0=col0)


def kernel(x_prompt, x_sample, cache_a1_k, cache_a1_v, cache_a2_k, cache_a2_v, cache_a3_k, cache_a3_v,
           cache_b_k, cache_b_v, cache_mem_k, cache_mem_v, page_table, mem_prompt,
           a_w_in, a_w_out, b_w_in, b_w_out, b_logit_bias, mem_w_q, mem_w_kv, mem_w_out, ln_g, ln_b,
           router_w, router_b, moe_w_gu, moe_b_gu, moe_w_down, moe_b_down):
    bp, sp, d = x_prompt.shape
    bs, ts, _ = x_sample.shape
    assert ts == 1 and bs == STEP_ROWS
    depth = ln_g.shape[0]
    alpha = (2 * depth) ** 0.25
    past_len = page_table.shape[1] * PAGE_SIZE
    a_heads = cache_a1_k.shape[3]
    a_gw = a_heads * HEAD_DIM
    b_heads = cache_b_k.shape[3]
    b_gw = b_heads * HEAD_DIM
    mem_len, mem_heads = cache_mem_k.shape[2], cache_mem_k.shape[3]
    mem_w = mem_heads * HEAD_DIM
    caches_k = (cache_a1_k, cache_a2_k, cache_a3_k)
    caches_v = (cache_a1_v, cache_a2_v, cache_a3_v)
    mem_rows = mem_prompt.reshape(bp * mem_len, d).astype(BF16)

    xp = x_prompt.reshape(bp * sp, d)
    xs = x_sample.reshape(bs, d)
    xpb, xsb = xp.astype(BF16), xs
    sorted_rows = jnp.zeros((_moe_sorted_rows(bp * sp + bs, router_w.shape[2]), d // 2), U32)
    pa_k, pa_v, sa_k, sa_v = ([[] for _ in range(3)] for _ in range(4))
    pb_k, pb_v, sb_k, sb_v, pm_k, pm_v = [], [], [], [], [], []

    for i in range(depth):
        j = i // 2
        if i % 2 == 0:
            qkv_p = _proj(xpb, a_w_in, j).reshape(bp, sp, -1)
            qkv_s = _proj(xsb, a_w_in, j).reshape(bs, -1, HEAD_DIM)
            op, *k_rots = _dilated_prompt(qkv_p, a_heads)
            op = op.reshape(bp * sp, a_gw)
            for g, win in enumerate(A_WINDOWS):
                keep = min(win, sp)
                v_g = qkv_p[:, :, (3 * g + 2) * a_gw:(3 * g + 3) * a_gw]
                pa_k[g].append(k_rots[g][:, sp - keep:].reshape(bp, keep, a_heads, HEAD_DIM))
                pa_v[g].append(v_g[:, sp - keep:].reshape(bp, keep, a_heads, HEAD_DIM))
            os_, k_new = _dilated_step(qkv_s, caches_k, caches_v, j, past_len)
            for g in range(3):
                sa_k[g].append(k_new[:, g * a_heads:(g + 1) * a_heads].reshape(bs, 1, a_heads, HEAD_DIM))
                sa_v[g].append(qkv_s[:, (3 * g + 2) * a_heads:(3 * g + 3) * a_heads]
                               .reshape(bs, 1, a_heads, HEAD_DIM))
            fp = _proj(op, a_w_out, j)
            fs = _proj(os_.reshape(bs, a_gw), a_w_out, j)
        else:
            qkv_p = _proj(xpb, b_w_in, j).reshape(bp, sp, -1)
            qkv_s = _proj(xsb, b_w_in, j).reshape(bs, 3, b_heads, HEAD_DIM)
            op = _sb_prompt(qkv_p, b_logit_bias[j], b_heads).reshape(bp * sp, b_gw)
            os_ = _sb_step(qkv_s[:, 0], b_logit_bias[j], cache_b_k, cache_b_v, j, page_table)
            pb_k.append(qkv_p[:, :, b_gw:2 * b_gw].reshape(bp, sp, b_heads, HEAD_DIM))
            pb_v.append(qkv_p[:, :, 2 * b_gw:].reshape(bp, sp, b_heads, HEAD_DIM))
            sb_k.append(qkv_s[:, 1].reshape(bs, 1, b_heads, HEAD_DIM))
            sb_v.append(qkv_s[:, 2].reshape(bs, 1, b_heads, HEAD_DIM))
            fp = _proj(op, b_w_out, j)
            fs = _proj(os_.reshape(bs, b_gw), b_w_out, j)
        xp, xpb = _deepnorm(xp, fp, ln_g[i, 0], ln_b[i, 0], alpha, tm=256)
        xs, xsb = _deepnorm(xs, fs, ln_g[i, 0], ln_b[i, 0], alpha, tm=bs)

        kv = _proj(mem_rows, mem_w_kv, i)
        kv4 = kv.reshape(bp, mem_len, 2, mem_heads, HEAD_DIM)
        pm_k.append(kv4[:, :, 0])
        pm_v.append(kv4[:, :, 1])
        qp = _proj(xpb, mem_w_q, i)
        qs = _proj(xsb, mem_w_q, i)
        op = _mem_attend_prompt(qp, kv, bp, tq=512)
        os_ = _mem_attend_step(qs.reshape(bs, mem_heads, HEAD_DIM), cache_mem_k, cache_mem_v, i)
        fp = _proj(op, mem_w_out, i)
        fs = _proj(os_.reshape(bs, mem_w), mem_w_out, i)
        xp, xpb, xp_packed = _deepnorm(xp, fp, ln_g[i, 1], ln_b[i, 1], alpha, tm=256, packed=True)
        xs, xsb, xs_packed = _deepnorm(xs, fs, ln_g[i, 1], ln_b[i, 1], alpha, tm=bs, packed=True)

        xp, xpb, xs, xsb, sorted_rows = _moe_layer(
            xp, xpb, xp_packed, xs, xs_packed, sorted_rows, i, router_w, router_b, moe_w_gu, moe_b_gu,
            moe_w_down, moe_b_down, ln_g[i, 2], ln_b[i, 2], alpha)

    stack = jnp.stack
    return (xp.reshape(bp, sp, d), xs.reshape(bs, ts, d),
            stack(pa_k[0]), stack(pa_v[0]), stack(pa_k[1]), stack(pa_v[1]), stack(pa_k[2]), stack(pa_v[2]),
            stack(pb_k), stack(pb_v), stack(pm_k), stack(pm_v),
            stack(sa_k[0]), stack(sa_v[0]), stack(sa_k[1]), stack(sa_v[1]), stack(sa_k[2]), stack(sa_v[2]),
            stack(sb_k), stack(sb_v))
```
